```python
import math
import jax, jax.numpy as jnp
from jax import lax
import numpy as np

D_MODEL = 1024
BATCH = 4
SEQ = 4096
DEPTH = 4
DEC_BATCH = 128
DEC_SEQ = 8
PAST_LEN = 2048
PAGE_SIZE = 128

D_MIX = D_MODEL
D_LRU = 3 * D_MODEL // 8
LRU_BLOCKS = 6
LRU_BLOCK = D_LRU // LRU_BLOCKS
CONV_W = 4
RG_C = 8.0
D_SSM = D_MODEL // 4
SSM_GROUP = 16
SSM_GROUPS = D_SSM // SSM_GROUP
SSM_STATE = 64
D_ATT = D_MIX - D_LRU - D_SSM
HEAD_DIM = 64
N_HEADS = D_ATT // HEAD_DIM
DILATED = ((128, 1), (512, 4), (2048, 16))
WINDOW_MAX = 2048
D_IN = 2 * D_LRU + 2 * D_SSM + 4 * D_ATT
SPLITS = (D_LRU, 2 * D_LRU, 2 * D_LRU + D_SSM, 2 * D_LRU + 2 * D_SSM,
          2 * D_LRU + 2 * D_SSM + D_ATT, 2 * D_LRU + 2 * D_SSM + 2 * D_ATT,
          2 * D_LRU + 2 * D_SSM + 3 * D_ATT)
EPS = 1e-6
NEG_INF = -1e30

kernel_name = 'hymba_style_rglru_s5_dilated_decoder_step'


def rms_norm(x, g):
    xf = x.astype(jnp.float32)
    y = xf * lax.rsqrt(jnp.mean(xf * xf, axis=-1, keepdims=True) + EPS)
    return (y * g.astype(jnp.float32)).astype(x.dtype)


def alibi_slopes():
    h = jnp.arange(1, N_HEADS + 1, dtype=jnp.float32)
    return jnp.exp2(-8.0 * h / N_HEADS)


def causal_conv(xa, buf, w, b):
    xp = jnp.concatenate([buf.astype(xa.dtype), xa], axis=1)
    y = lax.conv_general_dilated(xp, w[:, None, :].astype(xa.dtype), window_strides=(1,),
                                 padding='VALID', dimension_numbers=('NWC', 'WIO', 'NWC'),
                                 feature_group_count=xa.shape[-1])
    return y + b.astype(xa.dtype), xp[:, -(CONV_W - 1):]


def rg_lru(xc, h0, w_r, b_r, w_i, b_i, lam, seq_start):
    f32 = jnp.float32
    n, t, c = xc.shape
    xf = xc.astype(f32)
    xb = xf.reshape(n, t, LRU_BLOCKS, LRU_BLOCK)
    r = jax.nn.sigmoid(jnp.einsum('ntki,kij->ntkj', xb, w_r.astype(f32)).reshape(n, t, c) + b_r.astype(f32))
    i = jax.nn.sigmoid(jnp.einsum('ntki,kij->ntkj', xb, w_i.astype(f32)).reshape(n, t, c) + b_i.astype(f32))
    log_a = -RG_C * r * jax.nn.softplus(-lam.astype(f32))
    a = jnp.exp(log_a)
    mult = jnp.sqrt(-jnp.expm1(2.0 * log_a))
    if seq_start:
        mult = mult.at[:, 0].set(1.0)
    bx = mult * i * xf
    bx = bx.at[:, 0].add(a[:, 0] * h0.astype(f32))

    def comb(lhs, rhs):
        a1, b1 = lhs
        a2, b2 = rhs
        return a1 * a2, a2 * b1 + b2

    _, h = lax.associative_scan(comb, (a, bx), axis=1)
    return h, h[:, -1]


def s5_ssm(u, s0_re, s0_im, lam_re, lam_im, log_dt, b_re, b_im, c_re, c_im, d):
    f32 = jnp.float32
    n, t, _ = u.shape
    uf = u.astype(f32).reshape(n, t, SSM_GROUPS, SSM_GROUP)
    dt = jnp.exp(log_dt.astype(f32))
    lr, li = lam_re.astype(f32), lam_im.astype(f32)
    mag = jnp.exp(lr * dt)
    ang = li * dt
    ab_re, ab_im = mag * jnp.cos(ang), mag * jnp.sin(ang)
    den = lr * lr + li * li
    xr, yi = ab_re - 1.0, ab_im
    coef_re = (xr * lr + yi * li) / den
    coef_im = (yi * lr - xr * li) / den
    br, bi = b_re.astype(f32), b_im.astype(f32)
    bb_re = coef_re[..., None] * br - coef_im[..., None] * bi
    bb_im = coef_re[..., None] * bi + coef_im[..., None] * br
    bu_re = jnp.einsum('ntgc,gpc->ntgp', uf, bb_re)
    bu_im = jnp.einsum('ntgc,gpc->ntgp', uf, bb_im)
    s0r, s0i = s0_re.astype(f32), s0_im.astype(f32)
    bu_re = bu_re.at[:, 0].add(ab_re * s0r - ab_im * s0i)
    bu_im = bu_im.at[:, 0].add(ab_re * s0i + ab_im * s0r)
    a_re = jnp.broadcast_to(ab_re, bu_re.shape)
    a_im = jnp.broadcast_to(ab_im, bu_im.shape)

    def comb(lhs, rhs):
        ar1, ai1, br1, bi1 = lhs
        ar2, ai2, br2, bi2 = rhs
        return (ar2 * ar1 - ai2 * ai1, ar2 * ai1 + ai2 * ar1,
                ar2 * br1 - ai2 * bi1 + br2, ar2 * bi1 + ai2 * br1 + bi2)

    _, _, x_re, x_im = lax.associative_scan(comb, (a_re, a_im, bu_re, bu_im), axis=1)
    y = (jnp.einsum('ntgp,gcp->ntgc', x_re, c_re.astype(f32))
         - jnp.einsum('ntgp,gcp->ntgc', x_im, c_im.astype(f32)))
    y = y + d.astype(f32).reshape(SSM_GROUPS, SSM_GROUP) * uf
    return y.reshape(n, t, D_SSM), x_re[:, -1], x_im[:, -1]


def dilated_attn_prompt(q, k, v, window, dil, slopes):
    f32 = jnp.float32
    bsz, s, h, dh = q.shape
    jw = window // dil
    ln = s // dil
    nb = -(-ln // jw)
    lp = nb * jw

    def blocks(z):
        z = z.reshape(bsz, ln, dil, h, dh)
        z = jnp.pad(z, ((0, 0), (0, lp - ln), (0, 0), (0, 0), (0, 0)))
        return z.reshape(bsz, nb, jw, dil, h, dh)

    def with_prev(z):
        prev = jnp.pad(z[:, :-1], ((0, 0), (1, 0), (0, 0), (0, 0), (0, 0), (0, 0)))
        return jnp.concatenate([prev, z], axis=2)

    qb = blocks(q)
    kk, vv = with_prev(blocks(k)), with_prev(blocks(v))
    sc = jnp.einsum('bnqchd,bnkchd->bnchqk', qb, kk, preferred_element_type=f32)
    iq = jnp.arange(jw)[:, None]
    jk = jnp.arange(2 * jw)[None, :]
    delta = iq - jk + jw
    nblk = jnp.arange(nb)[:, None, None]
    valid = (delta >= 0) & (delta <= jw) & (nblk * jw - jw + jk >= 0)
    sc = sc - slopes[:, None, None] * (delta * dil).astype(f32)
    sc = jnp.where(valid[None, :, None, None], sc, NEG_INF)
    m = jnp.max(sc, axis=-1, keepdims=True)
    p = jnp.exp(sc - m)
    den = jnp.sum(p, axis=-1, keepdims=True)
    lse = (m + jnp.log(den))[..., 0]
    o = jnp.einsum('bnchqk,bnkchd->bnqchd', (p / den).astype(v.dtype), vv,
                   preferred_element_type=f32)
    o = o.reshape(bsz, lp, dil, h, dh)[:, :ln].reshape(bsz, s, h, dh)
    lse = jnp.transpose(lse, (0, 1, 4, 2, 3)).reshape(bsz, lp, dil, h)[:, :ln].reshape(bsz, s, h)
    return o, lse


def dilated_attn_sample(q, k_all, v_all, n_past, window, dil, slopes):
    f32 = jnp.float32
    t = q.shape[1]
    jw = window // dil
    dist = dil * jnp.arange(jw + 1)
    idx = n_past + jnp.arange(t)[:, None] - dist[None, :]
    valid = idx >= 0
    idx_c = jnp.maximum(idx, 0)
    kg = k_all[:, idx_c]
    vg = v_all[:, idx_c]
    sc = jnp.einsum('nthd,ntjhd->nthj', q, kg, preferred_element_type=f32)
    sc = sc - slopes[:, None] * dist.astype(f32)[None, :]
    sc = jnp.where(valid[None, :, None, :], sc, NEG_INF)
    m = jnp.max(sc, axis=-1, keepdims=True)
    p = jnp.exp(sc - m)
    den = jnp.sum(p, axis=-1, keepdims=True)
    lse = (m + jnp.log(den))[..., 0]
    o = jnp.einsum('nthj,ntjhd->nthd', (p / den).astype(vg.dtype), vg, preferred_element_type=f32)
    return o, lse


def combine_dilations(outs):
    w = jax.nn.softmax(jnp.stack([l for _, l in outs], axis=0), axis=0)
    o = w[0][..., None] * outs[0][0]
    for i in range(1, len(outs)):
        o = o + w[i][..., None] * outs[i][0]
    return o


def mixer_layer(x, p, seq_start, conv_buf, h0, s0_re, s0_im, k_past, v_past):
    n, t, _ = x.shape
    hN = rms_norm(x, p['norm_g'])
    z = hN @ p['w_in']
    xa, ga, u, gb, q, k, v, gc = jnp.split(z, SPLITS, axis=-1)
    xc, conv_new = causal_conv(xa, conv_buf, p['conv_w'], p['conv_b'])
    ha, h_last = rg_lru(xc, h0, p['w_r'], p['b_r'], p['w_i'], p['b_i'], p['lru_lambda'], seq_start)
    ya = ha.astype(x.dtype) * jax.nn.silu(ga)
    yb, s_re, s_im = s5_ssm(u, s0_re, s0_im, p['lam_re'], p['lam_im'], p['log_dt'],
                            p['b_re'], p['b_im'], p['c_re'], p['c_im'], p['d'])
    yb = jax.nn.gelu(yb)
    yb = yb * jax.nn.sigmoid(yb @ p['glu_w'].astype(jnp.float32) + p['glu_b'].astype(jnp.float32))
    yb = yb.astype(x.dtype) * jax.nn.silu(gb)
    q = rms_norm(q.reshape(n, t, N_HEADS, HEAD_DIM), p['q_g']) * (HEAD_DIM ** -0.5)
    k = rms_norm(k.reshape(n, t, N_HEADS, HEAD_DIM), p['k_g'])
    v = v.reshape(n, t, N_HEADS, HEAD_DIM)
    slopes = alibi_slopes()
    if k_past is None:
        outs = [dilated_attn_prompt(q, k, v, w, dl, slopes) for w, dl in DILATED]
        keep = min(WINDOW_MAX, t)
        k_rows, v_rows = k[:, -keep:], v[:, -keep:]
    else:
        n_past = k_past.shape[1]
        k_all = jnp.concatenate([k_past.astype(k.dtype), k], axis=1)
        v_all = jnp.concatenate([v_past.astype(v.dtype), v], axis=1)
        outs = [dilated_attn_sample(q, k_all, v_all, n_past, w, dl, slopes) for w, dl in DILATED]
        k_rows, v_rows = k, v
    yc = combine_dilations(outs).reshape(n, t, D_ATT).astype(x.dtype) * jax.nn.silu(gc)
    og = p['out_g']
    y = jnp.concatenate([rms_norm(ya, og[:D_LRU]),
                         rms_norm(yb, og[D_LRU:D_LRU + D_SSM]),
                         rms_norm(yc, og[D_LRU + D_SSM:])], axis=-1)
    x = x + y @ p['w_out']
    return x, (conv_new, h_last, s_re, s_im, k_rows, v_rows)


def setup_inputs(seed: int = 0) -> dict:
    key = jax.random.key(seed)
    ks = jax.random.split(key, 32)
    f32 = jnp.float32
    n_buf = min(WINDOW_MAX, PAST_LEN)

    def nrm(k, shape, s):
        return s * jax.random.normal(k, shape, f32)

    u_a = jax.random.uniform(ks[16], (DEPTH, D_LRU), f32, 0.9, 0.999)
    sig = u_a ** (1.0 / RG_C)
    lru_lambda = jnp.log(sig) - jnp.log1p(-sig)
    n_idx = jnp.arange(SSM_STATE, dtype=f32)
    lam_re = -0.5 * jnp.exp(nrm(ks[17], (DEPTH, SSM_GROUPS, SSM_STATE), 0.05))
    lam_im = math.pi * n_idx + nrm(ks[18], (DEPTH, SSM_GROUPS, SSM_STATE), 0.05)
    log_dt = jax.random.uniform(ks[19], (DEPTH, SSM_GROUPS, SSM_STATE), f32,
                                math.log(1e-3), math.log(1e-1))
    return {
        'x_prompt': nrm(ks[0], (BATCH, SEQ, D_MODEL), 1.0),
        'x_sample': nrm(ks[1], (DEC_BATCH, DEC_SEQ, D_MODEL), 1.0),
        'state_conv': nrm(ks[2], (DEPTH, DEC_BATCH, CONV_W - 1, D_LRU), 1.0),
        'state_lru': nrm(ks[3], (DEPTH, DEC_BATCH, D_LRU), 0.5),
        'state_ssm_re': nrm(ks[4], (DEPTH, DEC_BATCH, SSM_GROUPS, SSM_STATE), 0.1),
        'state_ssm_im': nrm(ks[5], (DEPTH, DEC_BATCH, SSM_GROUPS, SSM_STATE), 0.1),
        'cache_k': nrm(ks[6], (DEPTH, DEC_BATCH, n_buf, N_HEADS, HEAD_DIM), 1.0),
        'cache_v': nrm(ks[7], (DEPTH, DEC_BATCH, n_buf, N_HEADS, HEAD_DIM), 1.0),
        'norm_g': 1.0 + nrm(ks[8], (DEPTH, D_MODEL), 0.05),
        'w_in': nrm(ks[9], (DEPTH, D_MODEL, D_IN), D_MODEL ** -0.5),
        'conv_w': nrm(ks[10], (DEPTH, CONV_W, D_LRU), CONV_W ** -0.5),
        'conv_b': nrm(ks[11], (DEPTH, D_LRU), 0.02),
        'w_r': nrm(ks[12], (DEPTH, LRU_BLOCKS, LRU_BLOCK, LRU_BLOCK), LRU_BLOCK ** -0.5),
        'b_r': nrm(ks[13], (DEPTH, D_LRU), 0.02),
        'w_i': nrm(ks[14], (DEPTH, LRU_BLOCKS, LRU_BLOCK, LRU_BLOCK), LRU_BLOCK ** -0.5),
        'b_i': nrm(ks[15], (DEPTH, D_LRU), 0.02),
        'lru_lambda': lru_lambda,
        'ssm_lambda_re': lam_re,
        'ssm_lambda_im': lam_im,
        'ssm_log_dt': log_dt,
        'ssm_b_re': nrm(ks[20], (DEPTH, SSM_GROUPS, SSM_STATE, SSM_GROUP), (2 * SSM_GROUP) ** -0.5),
        'ssm_b_im': nrm(ks[21], (DEPTH, SSM_GROUPS, SSM_STATE, SSM_GROUP), (2 * SSM_GROUP) ** -0.5),
        'ssm_c_re': nrm(ks[22], (DEPTH, SSM_GROUPS, SSM_GROUP, SSM_STATE), SSM_STATE ** -0.5),
        'ssm_c_im': nrm(ks[23], (DEPTH, SSM_GROUPS, SSM_GROUP, SSM_STATE), SSM_STATE ** -0.5),
        'ssm_d': nrm(ks[24], (DEPTH, D_SSM), 1.0),
        'glu_w': nrm(ks[25], (DEPTH, D_SSM, D_SSM), D_SSM ** -0.5),
        'glu_b': nrm(ks[26], (DEPTH, D_SSM), 0.02),
        'q_norm_g': 1.0 + nrm(ks[27], (DEPTH, HEAD_DIM), 0.05),
        'k_norm_g': 1.0 + nrm(ks[28], (DEPTH, HEAD_DIM), 0.05),
        'out_norm_g': 1.0 + nrm(ks[29], (DEPTH, D_MIX), 0.05),
        'w_out': nrm(ks[30], (DEPTH, D_MIX, D_MODEL), 0.5 * D_MIX ** -0.5),
    }


def reference(x_prompt, x_sample, state_conv, state_lru, state_ssm_re, state_ssm_im, cache_k, cache_v,
              norm_g, w_in, conv_w, conv_b, w_r, b_r, w_i, b_i, lru_lambda,
              ssm_lambda_re, ssm_lambda_im, ssm_log_dt, ssm_b_re, ssm_b_im, ssm_c_re, ssm_c_im,
              ssm_d, glu_w, glu_b, q_norm_g, k_norm_g, out_norm_g, w_out):
    xp, xs = x_prompt, x_sample
    bp = xp.shape[0]
    zero_conv = jnp.zeros((bp, CONV_W - 1, D_LRU), xp.dtype)
    zero_h = jnp.zeros((bp, D_LRU), jnp.float32)
    zero_s = jnp.zeros((bp, SSM_GROUPS, SSM_STATE), jnp.float32)
    sp_list, ss_list = [], []
    for l in range(DEPTH):
        p = dict(norm_g=norm_g[l], w_in=w_in[l], conv_w=conv_w[l], conv_b=conv_b[l],
                 w_r=w_r[l], b_r=b_r[l], w_i=w_i[l], b_i=b_i[l], lru_lambda=lru_lambda[l],
                 lam_re=ssm_lambda_re[l], lam_im=ssm_lambda_im[l], log_dt=ssm_log_dt[l],
                 b_re=ssm_b_re[l], b_im=ssm_b_im[l], c_re=ssm_c_re[l], c_im=ssm_c_im[l],
                 d=ssm_d[l], glu_w=glu_w[l], glu_b=glu_b[l], q_g=q_norm_g[l], k_g=k_norm_g[l],
                 out_g=out_norm_g[l], w_out=w_out[l])
        xp, sp = mixer_layer(xp, p, True, zero_conv, zero_h, zero_s, zero_s, None, None)
        xs, ss = mixer_layer(xs, p, False, state_conv[l], state_lru[l], state_ssm_re[l],
                             state_ssm_im[l], cache_k[l], cache_v[l])
        sp_list.append(sp)
        ss_list.append(ss)
    conv_p = jnp.stack([s[0] for s in sp_list], axis=0)
    lru_p = jnp.stack([s[1] for s in sp_list], axis=0)
    ssm_re_p = jnp.stack([s[2] for s in sp_list], axis=0)
    ssm_im_p = jnp.stack([s[3] for s in sp_list], axis=0)
    k_p = jnp.stack([s[4] for s in sp_list], axis=0)
    v_p = jnp.stack([s[5] for s in sp_list], axis=0)
    conv_s = jnp.stack([s[0] for s in ss_list], axis=0)
    lru_s = jnp.stack([s[1] for s in ss_list], axis=0)
    ssm_re_s = jnp.stack([s[2] for s in ss_list], axis=0)
    ssm_im_s = jnp.stack([s[3] for s in ss_list], axis=0)
    k_s = jnp.stack([s[4] for s in ss_list], axis=0)
    v_s = jnp.stack([s[5] for s in ss_list], axis=0)
    return (xp, xs, conv_p, lru_p, ssm_re_p, ssm_im_p, k_p, v_p,
            conv_s, lru_s, ssm_re_s, ssm_im_s, k_s, v_s)
```

```python
import functools
import math

import jax
import jax.numpy as jnp
from jax import lax
from jax.experimental import pallas as pl
from jax.experimental.pallas import tpu as pltpu

F32 = jnp.float32
BF16 = jnp.bfloat16

D_MODEL = 1024
DEPTH = 4
PAST_LEN = 2048
D_LRU = 384
LRU_BLOCKS = 6
LRU_BLOCK = 64
CONV_W = 4
RG_C = 8.0
D_SSM = 256
SSM_GROUP = 16
SSM_GROUPS = 16
SSM_STATE = 64
N_STATE = SSM_GROUPS * SSM_STATE
D_ATT = 384
HEAD_DIM = 64
N_HEADS = 6
DILATED = ((128, 1), (512, 4), (2048, 16))
ATT_J = 128
D_IN = 2 * D_LRU + 2 * D_SSM + 4 * D_ATT
EPS = 1e-6
NEG_INF = -1e30

C_ZA = (0, 2 * D_LRU)
C_ZB = (2 * D_LRU, 2 * D_LRU + 2 * D_SSM)
C_Q = (C_ZB[1], C_ZB[1] + D_ATT)
C_K = (C_Q[1], C_Q[1] + D_ATT)
C_V = (C_K[1], C_K[1] + D_ATT)
C_GC = (C_V[1], C_V[1] + D_ATT)

V7X_VMEM_LIMIT = 56 * 1024 * 1024
LANES = 128
SCAN_R = 16
SCAN_ROWS = 32
ROW_TILE = 512
ATT_BLOCK = ATT_J


def _cparams(n_axes):
    return pltpu.CompilerParams(dimension_semantics=("arbitrary",) * n_axes,
                                vmem_limit_bytes=V7X_VMEM_LIMIT)


def _sigmoid(x):
    return 1.0 / (1.0 + jnp.exp(-x))


def _silu(x):
    return x * _sigmoid(x)


def _softplus(x):
    return jnp.maximum(x, 0.0) + jnp.log1p(jnp.exp(-jnp.abs(x)))


def _full(shape):
    nd = len(shape)
    return pl.BlockSpec(shape, lambda *_: (0,) * nd)


def _inproj_kernel(x_ref, g_ref, w_ref, qg_ref, kg_ref, e_ref,
                   za_ref, zb_ref, q_ref, k_ref, v_ref, gc_ref, kf_ref, vf_ref):
    x = x_ref[...]
    ms = jnp.mean(x * x, axis=-1, keepdims=True)
    hn = (x * lax.rsqrt(ms + EPS) * g_ref[...]).astype(BF16)

    def proj(cols):
        return jnp.dot(hn, w_ref[:, cols[0]:cols[1]], preferred_element_type=F32)

    def head_norm(z, gain):
        ss = jnp.dot((z * z).astype(BF16), e_ref[...], preferred_element_type=F32)
        return z * lax.rsqrt(ss * (1.0 / HEAD_DIM) + EPS) * gain

    za_ref[...] = proj(C_ZA)
    zb_ref[...] = proj(C_ZB)
    q = head_norm(proj(C_Q), qg_ref[...]) * (HEAD_DIM ** -0.5)
    k = head_norm(proj(C_K), kg_ref[...])
    v = proj(C_V)
    q_ref[...] = q.astype(BF16)
    k_ref[...] = k.astype(BF16)
    v_ref[...] = v.astype(BF16)
    kf_ref[...] = k
    vf_ref[...] = v
    gc_ref[...] = proj(C_GC)


def _inproj(x2, norm_g, w_in_bf, qg, kg, emat, seq_len, keep):
    n = x2.shape[0]
    tm = min(ROW_TILE, n)
    assert n % tm == 0 and (seq_len % tm == 0 or tm % seq_len == 0)
    nb = n // seq_len
    if seq_len >= tm:
        tiles_per_seq = seq_len // tm
        skip = (seq_len - keep) // tm
        assert (seq_len - keep) % tm == 0
        kv_shape = (nb, keep, D_ATT)
        kv_spec = pl.BlockSpec((None, tm, D_ATT),
                               lambda i: (i // tiles_per_seq,
                                          jnp.maximum(i % tiles_per_seq - skip, 0), 0))
    else:
        assert keep == seq_len
        kv_shape = (n, D_ATT)
        kv_spec = pl.BlockSpec((tm, D_ATT), lambda i: (i, 0))

    def rows(c):
        return pl.BlockSpec((tm, c), lambda i: (i, 0))

    outs = pl.pallas_call(
        _inproj_kernel,
        grid=(n // tm,),
        in_specs=[rows(D_MODEL), _full((1, D_MODEL)), _full((D_MODEL, D_IN)),
                  _full((1, D_ATT)), _full((1, D_ATT)), _full((D_ATT, D_ATT))],
        out_specs=[rows(2 * D_LRU), rows(2 * D_SSM), rows(D_ATT), rows(D_ATT), rows(D_ATT),
                   rows(D_ATT), kv_spec, kv_spec],
        out_shape=[jax.ShapeDtypeStruct((n, 2 * D_LRU), F32),
                   jax.ShapeDtypeStruct((n, 2 * D_SSM), F32),
                   jax.ShapeDtypeStruct((n, D_ATT), BF16),
                   jax.ShapeDtypeStruct((n, D_ATT), BF16),
                   jax.ShapeDtypeStruct((n, D_ATT), BF16),
                   jax.ShapeDtypeStruct((n, D_ATT), F32),
                   jax.ShapeDtypeStruct(kv_shape, F32),
                   jax.ShapeDtypeStruct(kv_shape, F32)],
        compiler_params=_cparams(1),
        name="inproj",
    )(x2, norm_g, w_in_bf, qg, kg, emat)
    return outs


def _lru_conv(xa, prev, cw_ref, cb_ref, xc_ref, r_steps):
    for s in range(r_steps):
        acc = cb_ref[...] + cw_ref[3:4, :] * xa(s)
        for back in (1, 2, 3):
            src = xa(s - back) if s - back >= 0 else prev[back - s]
            acc = acc + cw_ref[3 - back:4 - back, :] * src
        xc_ref[s] = acc


def _lru_gates(xc, wg_ref, bg_ref, lam_ref):
    g = jnp.dot(xc.astype(BF16), wg_ref[...], preferred_element_type=F32) + bg_ref[...]
    r = _sigmoid(g[:, :D_LRU])
    i = _sigmoid(g[:, D_LRU:])
    log_a = -RG_C * r * _softplus(-lam_ref[...])
    a = jnp.exp(log_a)
    mult = jnp.sqrt(-jnp.tanh(log_a) * (a * a + 1.0))
    return a, mult, i


def _lru_prompt_kernel(r_steps, nr, za_ref, cw_ref, cb_ref, wg_ref, bg_ref, lam_ref,
                       ya_ref, conv_ref, hl_ref,
                       xc_ref, a_ref, b_ref, cin_ref, cx_ref, ch_ref):
    c = D_LRU
    ti = pl.program_id(1)
    last = pl.num_programs(1) - 1

    @pl.when(ti == 0)
    def _():
        cx_ref[...] = jnp.zeros_like(cx_ref)
        ch_ref[...] = jnp.zeros_like(ch_ref)

    def xa(s):
        return za_ref[:, s * 2 * c:s * 2 * c + c]

    def ga(s):
        return za_ref[:, s * 2 * c + c:(s + 1) * 2 * c]

    row = lax.broadcasted_iota(jnp.int32, (nr, c), 0)
    prev = {j: jnp.where(row == 0, cx_ref[j - 1:j, :], pltpu.roll(xa(r_steps - j), 1, 0))
            for j in (1, 2, 3)}
    _lru_conv(xa, prev, cw_ref, cb_ref, xc_ref, r_steps)

    xc = xc_ref[...].reshape(r_steps * nr, c)
    a, mult, gate_i = _lru_gates(xc, wg_ref, bg_ref, lam_ref)
    flat_row = lax.broadcasted_iota(jnp.int32, (r_steps * nr, c), 0)
    mult = jnp.where(jnp.logical_and(flat_row == 0, ti == 0), 1.0, mult)
    a_ref[...] = a.reshape(r_steps, nr, c)
    b_ref[...] = (mult * gate_i * xc).reshape(r_steps, nr, c)

    h = jnp.zeros((nr, c), F32)
    p = jnp.ones((nr, c), F32)
    for s in range(r_steps):
        a_s = a_ref[s]
        h = a_s * h + b_ref[s]
        p = a_s * p
        b_ref[s] = h
        a_ref[s] = p

    carry = ch_ref[0:1, :]
    for i in range(nr):
        cin_ref[i:i + 1, :] = carry
        carry = a_ref[r_steps - 1, i:i + 1, :] * carry + b_ref[r_steps - 1, i:i + 1, :]
    ch_ref[0:1, :] = carry

    cin = cin_ref[...]
    for s in range(r_steps):
        h_s = b_ref[s] + a_ref[s] * cin
        ya_ref[:, s * c:(s + 1) * c] = h_s * _silu(ga(s))

    for j in (1, 2, 3):
        cx_ref[j - 1:j, :] = xa(r_steps - j)[nr - 1:nr, :]

    @pl.when(ti == last)
    def _():
        hl_ref[...] = carry
        for j in range(CONV_W - 1):
            conv_ref[j:j + 1, :] = xa(r_steps - (CONV_W - 1) + j)[nr - 1:nr, :]


def _lru_prompt(za, nb, seq_len, cw, cb, wg, bg, lam):
    r, nr, c = SCAN_R, SCAN_ROWS, D_LRU
    assert seq_len % (r * nr) == 0
    tiles = seq_len // (r * nr)
    zav = za.reshape(nb, seq_len // r, r * 2 * c)
    ya, conv_new, h_last = pl.pallas_call(
        functools.partial(_lru_prompt_kernel, r, nr),
        grid=(nb, tiles),
        in_specs=[pl.BlockSpec((None, nr, r * 2 * c), lambda b, t: (b, t, 0)),
                  _full((CONV_W, c)), _full((1, c)), _full((c, 2 * c)), _full((1, 2 * c)),
                  _full((1, c))],
        out_specs=[pl.BlockSpec((None, nr, r * c), lambda b, t: (b, t, 0)),
                   pl.BlockSpec((None, CONV_W - 1, c), lambda b, t: (b, 0, 0)),
                   pl.BlockSpec((None, 1, c), lambda b, t: (b, 0, 0))],
        out_shape=[jax.ShapeDtypeStruct((nb, seq_len // r, r * c), F32),
                   jax.ShapeDtypeStruct((nb, CONV_W - 1, c), F32),
                   jax.ShapeDtypeStruct((nb, 1, c), F32)],
        scratch_shapes=[pltpu.VMEM((r, nr, c), F32), pltpu.VMEM((r, nr, c), F32),
                        pltpu.VMEM((r, nr, c), F32), pltpu.VMEM((nr, c), F32),
                        pltpu.VMEM((8, c), F32), pltpu.VMEM((8, c), F32)],
        compiler_params=_cparams(2),
        name="lru_prompt",
    )(zav, cw, cb, wg, bg, lam)
    return ya.reshape(nb * seq_len, c), conv_new, h_last.reshape(nb, c)


def _lru_sample_kernel(r_steps, za_ref, sc_ref, h0_ref, cw_ref, cb_ref, wg_ref, bg_ref, lam_ref,
                       ya_ref, conv_ref, hl_ref, xc_ref, a_ref, b_ref):
    c = D_LRU
    nr = za_ref.shape[0]

    def xa(s):
        return za_ref[:, s * 2 * c:s * 2 * c + c]

    def ga(s):
        return za_ref[:, s * 2 * c + c:(s + 1) * 2 * c]

    prev = {j: sc_ref[:, (CONV_W - 1 - j) * c:(CONV_W - j) * c] for j in (1, 2, 3)}
    _lru_conv(xa, prev, cw_ref, cb_ref, xc_ref, r_steps)
    xc = xc_ref[...].reshape(r_steps * nr, c)
    a, mult, gate_i = _lru_gates(xc, wg_ref, bg_ref, lam_ref)
    a_ref[...] = a.reshape(r_steps, nr, c)
    b_ref[...] = (mult * gate_i * xc).reshape(r_steps, nr, c)
    h = h0_ref[...]
    for s in range(r_steps):
        h = a_ref[s] * h + b_ref[s]
        ya_ref[:, s * c:(s + 1) * c] = h * _silu(ga(s))
    hl_ref[...] = h
    for j in range(CONV_W - 1):
        conv_ref[:, j * c:(j + 1) * c] = xa(r_steps - (CONV_W - 1) + j)


def _lru_sample(za, nb, seq_len, state_conv, h0, cw, cb, wg, bg, lam):
    r, c = seq_len, D_LRU
    zav = za.reshape(nb, r * 2 * c)
    scv = state_conv.reshape(nb, (CONV_W - 1) * c)
    ya, conv_new, h_last = pl.pallas_call(
        functools.partial(_lru_sample_kernel, r),
        grid=(1,),
        in_specs=[_full((nb, r * 2 * c)), _full((nb, (CONV_W - 1) * c)), _full((nb, c)),
                  _full((CONV_W, c)), _full((1, c)), _full((c, 2 * c)), _full((1, 2 * c)),
                  _full((1, c))],
        out_specs=[_full((nb, r * c)), _full((nb, (CONV_W - 1) * c)), _full((nb, c))],
        out_shape=[jax.ShapeDtypeStruct((nb, r * c), F32),
                   jax.ShapeDtypeStruct((nb, (CONV_W - 1) * c), F32),
                   jax.ShapeDtypeStruct((nb, c), F32)],
        scratch_shapes=[pltpu.VMEM((r, nb, c), F32), pltpu.VMEM((r, nb, c), F32),
                        pltpu.VMEM((r, nb, c), F32)],
        compiler_params=_cparams(1),
        name="lru_sample",
    )(zav, scv, h0, cw, cb, wg, bg, lam)
    return ya.reshape(nb * seq_len, c), conv_new.reshape(nb, CONV_W - 1, c), h_last


def _ssm_disc_kernel(r_steps, lr_ref, li_ref, ldt_ref, br_ref, bi_ref,
                     ab_ref, bbr_ref, bbi_ref, apow_ref):
    lr, li = lr_ref[...], li_ref[...]
    dt = jnp.exp(ldt_ref[...])
    mag = jnp.exp(lr * dt)
    ang = li * dt
    ab_re, ab_im = mag * jnp.cos(ang), mag * jnp.sin(ang)
    den = lr * lr + li * li
    xr, yi = ab_re - 1.0, ab_im
    coef_re = (xr * lr + yi * li) / den
    coef_im = (yi * lr - xr * li) / den
    br, bi = br_ref[...], bi_ref[...]
    bbr_ref[...] = coef_re * br - coef_im * bi
    bbi_ref[...] = coef_re * bi + coef_im * br
    ab_ref[0:1, :] = ab_re
    ab_ref[1:2, :] = ab_im
    pr, pi = ab_re, ab_im
    for s in range(r_steps):
        apow_ref[s:s + 1, 0:N_STATE] = pr
        apow_ref[s:s + 1, N_STATE:2 * N_STATE] = pi
        pr, pi = pr * ab_re - pi * ab_im, pr * ab_im + pi * ab_re


def _ssm_discretise(lam_re, lam_im, log_dt, b_re, b_im):
    def vec():
        return pl.BlockSpec((None, 1, N_STATE), lambda l: (l, 0, 0))

    def mat():
        return pl.BlockSpec((None, SSM_GROUP, N_STATE), lambda l: (l, 0, 0))

    return pl.pallas_call(
        functools.partial(_ssm_disc_kernel, SCAN_R),
        grid=(DEPTH,),
        in_specs=[vec(), vec(), vec(), mat(), mat()],
        out_specs=[pl.BlockSpec((None, 2, N_STATE), lambda l: (l, 0, 0)), mat(), mat(),
                   pl.BlockSpec((None, SCAN_R, 2 * N_STATE), lambda l: (l, 0, 0))],
        out_shape=[jax.ShapeDtypeStruct((DEPTH, 2, N_STATE), F32),
                   jax.ShapeDtypeStruct((DEPTH, SSM_GROUP, N_STATE), F32),
                   jax.ShapeDtypeStruct((DEPTH, SSM_GROUP, N_STATE), F32),
                   jax.ShapeDtypeStruct((DEPTH, SCAN_R, 2 * N_STATE), F32)],
        compiler_params=_cparams(1),
        name="ssm_discretise",
    )(lam_re, lam_im, log_dt, b_re, b_im)


def _ssm_readout(x_all, u_all, g_all, cmat_ref, d_ref, gw_ref, gb_ref):
    y = jnp.dot(x_all, cmat_ref[...], preferred_element_type=F32) + d_ref[...] * u_all
    y = jax.nn.gelu(y)
    y = y * _sigmoid(jnp.dot(y.astype(BF16), gw_ref[...], preferred_element_type=F32) + gb_ref[...])
    return y * _silu(g_all)


def _ssm_prompt_kernel(r_steps, nr, zb_ref, bmat_ref, cmat_ref, ab_ref, apow_ref, d_ref,
                       gw_ref, gb_ref, yb_ref, sre_ref, sim_ref,
                       u_ref, g_ref, xs_ref, xb_ref, cin_ref, cst_ref):
    n, c = N_STATE, D_SSM
    ti = pl.program_id(1)
    last = pl.num_programs(1) - 1

    @pl.when(ti == 0)
    def _():
        cst_ref[...] = jnp.zeros_like(cst_ref)

    for s in range(r_steps):
        u_ref[s] = zb_ref[:, s * 2 * c:s * 2 * c + c]
        g_ref[s] = zb_ref[:, s * 2 * c + c:(s + 1) * 2 * c]
    u_all = u_ref[...].reshape(r_steps * nr, c)
    xs_ref[...] = jnp.dot(u_all.astype(BF16), bmat_ref[...],
                          preferred_element_type=F32).reshape(r_steps, nr, 2 * n)

    lane_block = 2 * LANES
    for lo in range(0, n, lane_block):
        hi = lo + lane_block
        ar, ai = ab_ref[0:1, lo:hi], ab_ref[1:2, lo:hi]
        xr = jnp.zeros((nr, lane_block), F32)
        xi = jnp.zeros((nr, lane_block), F32)
        for s in range(r_steps):
            nxr = ar * xr - ai * xi + xs_ref[s, :, lo:hi]
            nxi = ar * xi + ai * xr + xs_ref[s, :, n + lo:n + hi]
            xr, xi = nxr, nxi
            xs_ref[s, :, lo:hi] = xr
            xs_ref[s, :, n + lo:n + hi] = xi

    pr_end, pi_end = apow_ref[r_steps - 1:r_steps, 0:n], apow_ref[r_steps - 1:r_steps, n:2 * n]
    cr, ci = cst_ref[0:1, 0:n], cst_ref[0:1, n:2 * n]
    for i in range(nr):
        cin_ref[i:i + 1, 0:n] = cr
        cin_ref[i:i + 1, n:2 * n] = ci
        er = xs_ref[r_steps - 1, i:i + 1, 0:n]
        ei = xs_ref[r_steps - 1, i:i + 1, n:2 * n]
        cr, ci = pr_end * cr - pi_end * ci + er, pr_end * ci + pi_end * cr + ei
    cst_ref[0:1, 0:n] = cr
    cst_ref[0:1, n:2 * n] = ci

    for s in range(r_steps):
        pr, pi = apow_ref[s:s + 1, 0:n], apow_ref[s:s + 1, n:2 * n]
        cinr, cini = cin_ref[:, 0:n], cin_ref[:, n:2 * n]
        xb_ref[s, :, 0:n] = (xs_ref[s, :, 0:n] + pr * cinr - pi * cini).astype(BF16)
        xb_ref[s, :, n:2 * n] = (xs_ref[s, :, n:2 * n] + pr * cini + pi * cinr).astype(BF16)

    yb = _ssm_readout(xb_ref[...].reshape(r_steps * nr, 2 * n), u_all,
                      g_ref[...].reshape(r_steps * nr, c), cmat_ref, d_ref, gw_ref, gb_ref)
    for s in range(r_steps):
        yb_ref[:, s * c:(s + 1) * c] = yb[s * nr:(s + 1) * nr]

    @pl.when(ti == last)
    def _():
        sre_ref[...] = cr
        sim_ref[...] = ci


def _ssm_prompt(zb, nb, seq_len, bmat, cmat, ab, apow, d, gw, gb):
    r, nr, c, n = SCAN_R, SCAN_ROWS, D_SSM, N_STATE
    tiles = seq_len // (r * nr)
    zbv = zb.reshape(nb, seq_len // r, r * 2 * c)
    yb, s_re, s_im = pl.pallas_call(
        functools.partial(_ssm_prompt_kernel, r, nr),
        grid=(nb, tiles),
        in_specs=[pl.BlockSpec((None, nr, r * 2 * c), lambda b, t: (b, t, 0)),
                  _full((c, 2 * n)), _full((2 * n, c)), _full((2, n)), _full((r, 2 * n)),
                  _full((1, c)), _full((c, c)), _full((1, c))],
        out_specs=[pl.BlockSpec((None, nr, r * c), lambda b, t: (b, t, 0)),
                   pl.BlockSpec((None, 1, n), lambda b, t: (b, 0, 0)),
                   pl.BlockSpec((None, 1, n), lambda b, t: (b, 0, 0))],
        out_shape=[jax.ShapeDtypeStruct((nb, seq_len // r, r * c), F32),
                   jax.ShapeDtypeStruct((nb, 1, n), F32),
                   jax.ShapeDtypeStruct((nb, 1, n), F32)],
        scratch_shapes=[pltpu.VMEM((r, nr, c), F32), pltpu.VMEM((r, nr, c), F32),
                        pltpu.VMEM((r, nr, 2 * n), F32), pltpu.VMEM((r, nr, 2 * n), BF16),
                        pltpu.VMEM((nr, 2 * n), F32), pltpu.VMEM((8, 2 * n), F32)],
        compiler_params=_cparams(2),
        name="ssm_prompt",
    )(zbv, bmat, cmat, ab, apow, d, gw, gb)
    return yb.reshape(nb * seq_len, c), s_re.reshape(nb, n), s_im.reshape(nb, n)


def _ssm_sample_kernel(r_steps, zb_ref, s0r_ref, s0i_ref, bmat_ref, cmat_ref, ab_ref, d_ref,
                       gw_ref, gb_ref, yb_ref, sre_ref, sim_ref, u_ref, g_ref, xs_ref, xb_ref):
    n, c = N_STATE, D_SSM
    nr = zb_ref.shape[0]
    for s in range(r_steps):
        u_ref[s] = zb_ref[:, s * 2 * c:s * 2 * c + c]
        g_ref[s] = zb_ref[:, s * 2 * c + c:(s + 1) * 2 * c]
    u_all = u_ref[...].reshape(r_steps * nr, c)
    xs_ref[...] = jnp.dot(u_all.astype(BF16), bmat_ref[...],
                          preferred_element_type=F32).reshape(r_steps, nr, 2 * n)
    for lo in range(0, n, LANES):
        hi = lo + LANES
        ar, ai = ab_ref[0:1, lo:hi], ab_ref[1:2, lo:hi]
        xr, xi = s0r_ref[:, lo:hi], s0i_ref[:, lo:hi]
        for s in range(r_steps):
            nxr = ar * xr - ai * xi + xs_ref[s, :, lo:hi]
            nxi = ar * xi + ai * xr + xs_ref[s, :, n + lo:n + hi]
            xr, xi = nxr, nxi
            xb_ref[s, :, lo:hi] = xr.astype(BF16)
            xb_ref[s, :, n + lo:n + hi] = xi.astype(BF16)
        sre_ref[:, lo:hi] = xr
        sim_ref[:, lo:hi] = xi
    yb = _ssm_readout(xb_ref[...].reshape(r_steps * nr, 2 * n), u_all,
                      g_ref[...].reshape(r_steps * nr, c), cmat_ref, d_ref, gw_ref, gb_ref)
    for s in range(r_steps):
        yb_ref[:, s * c:(s + 1) * c] = yb[s * nr:(s + 1) * nr]


def _ssm_sample(zb, nb, seq_len, s0_re, s0_im, bmat, cmat, ab, d, gw, gb):
    r, c, n = seq_len, D_SSM, N_STATE
    zbv = zb.reshape(nb, r * 2 * c)
    yb, s_re, s_im = pl.pallas_call(
        functools.partial(_ssm_sample_kernel, r),
        grid=(1,),
        in_specs=[_full((nb, r * 2 * c)), _full((nb, n)), _full((nb, n)),
                  _full((c, 2 * n)), _full((2 * n, c)), _full((2, n)),
                  _full((1, c)), _full((c, c)), _full((1, c))],
        out_specs=[_full((nb, r * c)), _full((nb, n)), _full((nb, n))],
        out_shape=[jax.ShapeDtypeStruct((nb, r * c), F32),
                   jax.ShapeDtypeStruct((nb, n), F32),
                   jax.ShapeDtypeStruct((nb, n), F32)],
        scratch_shapes=[pltpu.VMEM((r, nb, c), F32), pltpu.VMEM((r, nb, c), F32),
                        pltpu.VMEM((r, nb, 2 * n), F32), pltpu.VMEM((r, nb, 2 * n), BF16)],
        compiler_params=_cparams(1),
        name="ssm_sample",
    )(zbv, s0_re, s0_im, bmat, cmat, ab, d, gw, gb)
    return yb.reshape(nb * seq_len, c), s_re, s_im


def _attn_prompt_kernel(q_ref, k_ref, v_ref, bias_ref, o_ref, l_ref, kk_ref, vv_ref):
    blk = ATT_BLOCK
    jb = pl.program_id(2)

    @pl.when(jb == 0)
    def _():
        kk_ref[0:blk, :] = jnp.zeros((blk, D_ATT), BF16)
        vv_ref[0:blk, :] = jnp.zeros((blk, D_ATT), BF16)

    kk_ref[blk:2 * blk, :] = k_ref[...]
    vv_ref[blk:2 * blk, :] = v_ref[...]

    lane = lax.broadcasted_iota(jnp.int32, (1, LANES), 1)
    col = lax.broadcasted_iota(jnp.int32, (1, 2 * blk), 1)
    pen = jnp.where(jnp.logical_and(col < blk, jb == 0), NEG_INF, 0.0).astype(F32)
    for pair in range(N_HEADS // 2):
        sl = slice(pair * LANES, (pair + 1) * LANES)
        qp, kp, vp = q_ref[:, sl], kk_ref[:, sl], vv_ref[:, sl]
        acc = jnp.zeros((blk, LANES), F32)
        lse = jnp.zeros((blk, LANES), F32)
        for hh in range(2):
            h = 2 * pair + hh
            head_lanes = (lane >= HEAD_DIM) if hh else (lane < HEAD_DIM)
            qh = jnp.where(head_lanes, qp, jnp.zeros_like(qp))
            s = lax.dot_general(qh, kp, (((1,), (1,)), ((), ())), preferred_element_type=F32)
            s = s + bias_ref[h] + pen
            m = jnp.max(s, axis=-1, keepdims=True)
            e = jnp.exp(s - m)
            den = jnp.sum(e, axis=-1, keepdims=True)
            vh = jnp.where(head_lanes, vp, jnp.zeros_like(vp))
            pv = jnp.dot(e.astype(BF16), vh, preferred_element_type=F32)
            acc = acc + pv * (1.0 / den)
            lse = jnp.where(head_lanes, m + jnp.log(den), lse)
        o_ref[:, sl] = acc
        l_ref[:, sl] = lse

    kk_ref[0:blk, :] = k_ref[...]
    vv_ref[0:blk, :] = v_ref[...]


def _prompt_bias(dil, slopes):
    iq = jnp.arange(ATT_BLOCK)[:, None]
    jk = jnp.arange(2 * ATT_BLOCK)[None, :]
    delta = iq - jk + ATT_BLOCK
    valid = (delta >= 0) & (delta <= ATT_J)
    bias = -slopes[:, None, None] * (delta * dil).astype(F32)[None]
    return jnp.where(valid[None], bias, NEG_INF).astype(F32)


def _attn_prompt(q, k, v, nb, seq_len, dil, slopes):
    blk = ATT_BLOCK
    rows = seq_len // dil
    assert rows % blk == 0

    def view(a):
        return a.reshape(nb, rows, dil * D_ATT)

    spec = pl.BlockSpec((None, blk, D_ATT), lambda b, r, j: (b, j, r))
    o, lse = pl.pallas_call(
        _attn_prompt_kernel,
        grid=(nb, dil, rows // blk),
        in_specs=[spec, spec, spec, _full((N_HEADS, blk, 2 * blk))],
        out_specs=[spec, spec],
        out_shape=[jax.ShapeDtypeStruct((nb, rows, dil * D_ATT), F32),
                   jax.ShapeDtypeStruct((nb, rows, dil * D_ATT), F32)],
        scratch_shapes=[pltpu.VMEM((2 * blk, D_ATT), BF16), pltpu.VMEM((2 * blk, D_ATT), BF16)],
        compiler_params=_cparams(3),
        name="attn_prompt_d%d" % dil,
    )(view(q), view(k), view(v), _prompt_bias(dil, slopes))
    return o.reshape(nb * seq_len, D_ATT), lse.reshape(nb * seq_len, D_ATT)


CACHE_GROUP = 16
N_GROUPS = PAST_LEN // CACHE_GROUP
HALF = CACHE_GROUP // 2
NEAR_GROUPS = DILATED[1][0] // CACHE_GROUP
ROWS_A = N_GROUPS * HALF
ROWS_B = NEAR_GROUPS * HALF
ROWS_ALL = ROWS_A + ROWS_B + LANES


def _attn_sample_kernel(bs, t_new, q_ref, kn_ref, vn_ref, ka_ref, kb_ref, va_ref, vb_ref, bias_ref,
                        o_ref, kc_ref, vc_ref):
    rows_qh = N_HEADS * t_new
    row_head = lax.broadcasted_iota(jnp.int32, (rows_qh, D_ATT), 0) // t_new
    lane_head = lax.broadcasted_iota(jnp.int32, (rows_qh, D_ATT), 1) // HEAD_DIM
    own = row_head == lane_head
    pad = jnp.zeros((LANES - t_new, D_ATT), F32)
    for b in range(bs):
        kc_ref[0:ROWS_A, :] = ka_ref[b].reshape(ROWS_A, D_ATT).astype(BF16)
        kc_ref[ROWS_A:ROWS_A + ROWS_B, :] = kb_ref[b].reshape(ROWS_B, D_ATT).astype(BF16)
        kc_ref[ROWS_A + ROWS_B:ROWS_ALL, :] = jnp.concatenate([kn_ref[b], pad], axis=0).astype(BF16)
        vc_ref[0:ROWS_A, :] = va_ref[b].reshape(ROWS_A, D_ATT).astype(BF16)
        vc_ref[ROWS_A:ROWS_A + ROWS_B, :] = vb_ref[b].reshape(ROWS_B, D_ATT).astype(BF16)
        vc_ref[ROWS_A + ROWS_B:ROWS_ALL, :] = jnp.concatenate([vn_ref[b], pad], axis=0).astype(BF16)
        q8 = q_ref[b].astype(F32)
        qbd = jnp.where(own, jnp.concatenate([q8] * N_HEADS, axis=0), 0.0).astype(BF16)
        s = lax.dot_general(qbd, kc_ref[...], (((1,), (1,)), ((), ())), preferred_element_type=F32)
        es, lses = [], []
        for p in range(len(DILATED)):
            sp = s + bias_ref[p]
            m = jnp.max(sp, axis=-1, keepdims=True)
            e = jnp.exp(sp - m)
            den = jnp.sum(e, axis=-1, keepdims=True)
            es.append(e / den)
            lses.append(m + jnp.log(den))
        top = jnp.maximum(jnp.maximum(lses[0], lses[1]), lses[2])
        ws = [jnp.exp(l - top) for l in lses]
        wsum = ws[0] + ws[1] + ws[2]
        pc = (ws[0] * es[0] + ws[1] * es[1] + ws[2] * es[2]) / wsum
        o_all = jnp.dot(pc.astype(BF16), vc_ref[...], preferred_element_type=F32)
        o_all = jnp.where(own, o_all, 0.0)
        out = o_all[0:t_new]
        for h in range(1, N_HEADS):
            out = out + o_all[h * t_new:(h + 1) * t_new]
        o_ref[b] = out


def _sample_bias(t_new, slopes):
    col = jnp.arange(ROWS_ALL)
    in_a = col < ROWS_A
    in_b = (col >= ROWS_A) & (col < ROWS_A + ROWS_B)
    in_n = (col >= ROWS_A + ROWS_B) & (col < ROWS_A + ROWS_B + t_new)
    cb = col - ROWS_A
    pos = jnp.where(in_a, (col // HALF) * CACHE_GROUP + col % HALF,
                    jnp.where(in_b, (N_GROUPS - NEAR_GROUPS + cb // HALF) * CACHE_GROUP + HALF + cb % HALF,
                              PAST_LEN + (col - ROWS_A - ROWS_B)))
    real = in_a | in_b | in_n
    t = jnp.arange(t_new)
    dist = PAST_LEN + t[:, None] - pos[None, :]
    out = []
    for window, dil in DILATED:
        valid = real[None, :] & (dist >= 0) & (dist % dil == 0) & (dist <= window)
        bias = -slopes[:, None, None] * dist.astype(F32)[None]
        bias = jnp.where(valid[None], bias, NEG_INF)
        out.append(bias.reshape(N_HEADS * t_new, ROWS_ALL))
    return jnp.stack(out, axis=0).astype(F32)


def _attn_sample(q, k_new, v_new, cache_k, cache_v, layer, nb, t_new, slopes):
    bs = 2
    assert nb % bs == 0 and cache_k.shape[2] == PAST_LEN
    ck = cache_k.reshape(DEPTH, nb, N_GROUPS, CACHE_GROUP, D_ATT)
    cv = cache_v.reshape(DEPTH, nb, N_GROUPS, CACHE_GROUP, D_ATT)
    spec_a = pl.BlockSpec((None, bs, N_GROUPS, HALF, D_ATT), lambda i: (layer, i, 0, 0, 0))
    spec_b = pl.BlockSpec((None, bs, NEAR_GROUPS, HALF, D_ATT),
                          lambda i: (layer, i, N_GROUPS // NEAR_GROUPS - 1, 1, 0))
    spec_new = pl.BlockSpec((bs, t_new, D_ATT), lambda i: (i, 0, 0))
    o = pl.pallas_call(
        functools.partial(_attn_sample_kernel, bs, t_new),
        grid=(nb // bs,),
        in_specs=[spec_new, spec_new, spec_new, spec_a, spec_b, spec_a, spec_b,
                  _full((len(DILATED), N_HEADS * t_new, ROWS_ALL))],
        out_specs=spec_new,
        out_shape=jax.ShapeDtypeStruct((nb, t_new, D_ATT), F32),
        scratch_shapes=[pltpu.VMEM((ROWS_ALL, D_ATT), BF16), pltpu.VMEM((ROWS_ALL, D_ATT), BF16)],
        compiler_params=_cparams(1),
        name="attn_sample",
    )(q.reshape(nb, t_new, D_ATT), k_new.reshape(nb, t_new, D_ATT), v_new.reshape(nb, t_new, D_ATT),
      ck, ck, cv, cv, _sample_bias(t_new, slopes))
    return o.reshape(nb * t_new, D_ATT)


def _outproj_kernel(n_pat, x_ref, ya_ref, yb_ref, gc_ref, *refs):
    n_lse = n_pat if n_pat > 1 else 0
    o_refs, l_refs = refs[:n_pat], refs[n_pat:n_pat + n_lse]
    og_ref, w_ref, out_ref = refs[n_pat + n_lse:]
    if n_pat == 1:
        o = o_refs[0][...]
    else:
        lses = [r[...] for r in l_refs]
        top = functools.reduce(jnp.maximum, lses)
        ws = [jnp.exp(l - top) for l in lses]
        o = functools.reduce(lambda a, b: a + b, [w * r[...] for w, r in zip(ws, o_refs)])
        o = o / functools.reduce(lambda a, b: a + b, ws)
    yc = o * _silu(gc_ref[...])

    def rms(y, lo, hi):
        yn = y * lax.rsqrt(jnp.mean(y * y, axis=-1, keepdims=True) + EPS) * og_ref[:, lo:hi]
        return jnp.dot(yn.astype(BF16), w_ref[lo:hi, :], preferred_element_type=F32)

    a_hi, b_hi = D_LRU, D_LRU + D_SSM
    out_ref[...] = (x_ref[...] + rms(ya_ref[...], 0, a_hi) + rms(yb_ref[...], a_hi, b_hi)
                    + rms(yc, b_hi, D_MODEL))


def _outproj(x2, ya, yb, gc, outs, lses, og, w_out_bf):
    n = x2.shape[0]
    tm = min(ROW_TILE, n)
    n_pat = len(outs)

    def rows(c):
        return pl.BlockSpec((tm, c), lambda i: (i, 0))

    extra = list(outs) + list(lses)
    return pl.pallas_call(
        functools.partial(_outproj_kernel, n_pat),
        grid=(n // tm,),
        in_specs=[rows(D_MODEL), rows(D_LRU), rows(D_SSM), rows(D_ATT)]
                 + [rows(D_ATT)] * len(extra) + [_full((1, D_MODEL)), _full((D_MODEL, D_MODEL))],
        out_specs=rows(D_MODEL),
        out_shape=jax.ShapeDtypeStruct((n, D_MODEL), F32),
        compiler_params=_cparams(1),
        name="outproj",
    )(x2, ya, yb, gc, *extra, og, w_out_bf)


def _block_diag(blocks):
    k, i, j = blocks.shape
    eye = jnp.eye(k, dtype=blocks.dtype)
    return jnp.einsum("kij,kl->kilj", blocks, eye).reshape(k * i, k * j)


def kernel(x_prompt, x_sample, state_conv, state_lru, state_ssm_re, state_ssm_im, cache_k, cache_v,
           norm_g, w_in, conv_w, conv_b, w_r, b_r, w_i, b_i, lru_lambda,
           ssm_lambda_re, ssm_lambda_im, ssm_log_dt, ssm_b_re, ssm_b_im, ssm_c_re, ssm_c_im,
           ssm_d, glu_w, glu_b, q_norm_g, k_norm_g, out_norm_g, w_out):
    bp, seq, _ = x_prompt.shape
    bs_, t_new, _ = x_sample.shape
    keep = min(PAST_LEN, seq)
    xp = x_prompt.reshape(bp * seq, D_MODEL)
    xs = x_sample.reshape(bs_ * t_new, D_MODEL)

    h = jnp.arange(1, N_HEADS + 1, dtype=F32)
    slopes = jnp.exp2(-8.0 * h / N_HEADS)
    head_of = jnp.arange(D_ATT) // HEAD_DIM
    emat = (head_of[:, None] == head_of[None, :]).astype(BF16)

    def chan_major(b):
        return jnp.transpose(b, (0, 3, 1, 2)).reshape(DEPTH, SSM_GROUP, N_STATE)

    def vec(a):
        return a.reshape(DEPTH, 1, N_STATE)

    ab_all, bbr_all, bbi_all, apow_all = _ssm_discretise(
        vec(ssm_lambda_re), vec(ssm_lambda_im), vec(ssm_log_dt), chan_major(ssm_b_re), chan_major(ssm_b_im))

    sp_list, ss_list = [], []
    for l in range(DEPTH):
        w_in_bf = w_in[l].astype(BF16)
        w_out_bf = w_out[l].astype(BF16)
        ng = norm_g[l].reshape(1, D_MODEL)
        qg = jnp.tile(q_norm_g[l], N_HEADS).reshape(1, D_ATT)
        kg = jnp.tile(k_norm_g[l], N_HEADS).reshape(1, D_ATT)
        og = out_norm_g[l].reshape(1, D_MODEL)
        cw, cb = conv_w[l], conv_b[l].reshape(1, D_LRU)
        wg = jnp.concatenate([_block_diag(w_r[l]), _block_diag(w_i[l])], axis=1).astype(BF16)
        bg = jnp.concatenate([b_r[l], b_i[l]]).reshape(1, 2 * D_LRU)
        lam = lru_lambda[l].reshape(1, D_LRU)

        def b_matrix(bb):
            blocks = jnp.transpose(bb.reshape(SSM_GROUP, SSM_GROUPS, SSM_STATE), (1, 0, 2))
            return _block_diag(blocks)
        bmat = jnp.concatenate([b_matrix(bbr_all[l]), b_matrix(bbi_all[l])], axis=1).astype(BF16)
        cmat = jnp.concatenate([_block_diag(jnp.transpose(ssm_c_re[l], (0, 2, 1))),
                                -_block_diag(jnp.transpose(ssm_c_im[l], (0, 2, 1)))], axis=0).astype(BF16)
        d = ssm_d[l].reshape(1, D_SSM)
        gw = glu_w[l].astype(BF16)
        gb = glu_b[l].reshape(1, D_SSM)

        za, zb, q, k, v, gc, k_rows, v_rows = _inproj(xp, ng, w_in_bf, qg, kg, emat, seq, keep)
        ya, conv_new, h_last = _lru_prompt(za, bp, seq, cw, cb, wg, bg, lam)
        yb, s_re, s_im = _ssm_prompt(zb, bp, seq, bmat, cmat, ab_all[l], apow_all[l], d, gw, gb)
        pats = [_attn_prompt(q, k, v, bp, seq, dil, slopes) for _, dil in DILATED]
        xp = _outproj(xp, ya, yb, gc, [o for o, _ in pats], [s for _, s in pats], og, w_out_bf)
        sp_list.append((conv_new, h_last,
                        s_re.reshape(bp, SSM_GROUPS, SSM_STATE), s_im.reshape(bp, SSM_GROUPS, SSM_STATE),
                        k_rows.reshape(bp, keep, N_HEADS, HEAD_DIM), v_rows.reshape(bp, keep, N_HEADS, HEAD_DIM)))

        za, zb, q, k, v, gc, k_rows, v_rows = _inproj(xs, ng, w_in_bf, qg, kg, emat, t_new, t_new)
        ya, conv_new, h_last = _lru_sample(za, bs_, t_new, state_conv[l], state_lru[l], cw, cb, wg, bg, lam)
        yb, s_re, s_im = _ssm_sample(zb, bs_, t_new, state_ssm_re[l].reshape(bs_, N_STATE),
                                     state_ssm_im[l].reshape(bs_, N_STATE), bmat, cmat, ab_all[l], d, gw, gb)
        o = _attn_sample(q, k_rows, v_rows, cache_k, cache_v, l, bs_, t_new, slopes)
        xs = _outproj(xs, ya, yb, gc, [o], [], og, w_out_bf)
        ss_list.append((conv_new, h_last,
                        s_re.reshape(bs_, SSM_GROUPS, SSM_STATE), s_im.reshape(bs_, SSM_GROUPS, SSM_STATE),
                        k_rows.reshape(bs_, t_new, N_HEADS, HEAD_DIM), v_rows.reshape(bs_, t_new, N_HEADS, HEAD_DIM)))

    def stack(lst, i):
        return jnp.stack([s[i] for s in lst], axis=0)

    return (xp.reshape(bp, seq, D_MODEL), xs.reshape(bs_, t_new, D_MODEL),
            stack(sp_list, 0), stack(sp_list, 1), stack(sp_list, 2), stack(sp_list, 3),
            stack(sp_list, 4), stack(sp_list, 5),
            stack(ss_list, 0), stack(ss_list, 1), stack(ss_list, 2), stack(ss_list, 3),
            stack(ss_list, 4), stack(ss_list, 5))
```

```python
import functools

import jax
import jax.numpy as jnp
from jax import lax
from jax.experimental import pallas as pl
from jax.experimental.pallas import tpu as pltpu

F32 = jnp.float32
BF16 = jnp.bfloat16

D_MODEL = 1024
DEPTH = 4
PAST_LEN = 2048
D_LRU = 384
CONV_W = 4
RG_C = 8.0
D_SSM = 256
SSM_GROUP = 16
SSM_GROUPS = 16
SSM_STATE = 64
N_STATE = SSM_GROUPS * SSM_STATE
D_ATT = 384
HEAD_DIM = 64
N_HEADS = 6
DILATED = ((128, 1), (512, 4), (2048, 16))
ATT_J = 128
D_IN = 2 * D_LRU + 2 * D_SSM + 4 * D_ATT
EPS = 1e-6
NEG_INF = -1e30

C_ZA = (0, 2 * D_LRU)
C_ZB = (2 * D_LRU, 2 * D_LRU + 2 * D_SSM)
C_Q = (C_ZB[1], C_ZB[1] + D_ATT)
C_K = (C_Q[1], C_Q[1] + D_ATT)
C_V = (C_K[1], C_K[1] + D_ATT)
C_GC = (C_V[1], C_V[1] + D_ATT)

V7X_VMEM_LIMIT = 56 * 1024 * 1024
LANES = 128
CHUNK = 16
CHUNK_ROWS = 32
ROW_TILE = 512
ATT_BLOCK = ATT_J


def _cparams(n_axes):
    return pltpu.CompilerParams(dimension_semantics=("arbitrary",) * n_axes,
                                vmem_limit_bytes=V7X_VMEM_LIMIT)


def _sigmoid(x):
    return 1.0 / (1.0 + jnp.exp(-x))


def _silu(x):
    return x * _sigmoid(x)


def _softplus(x):
    return jnp.maximum(x, 0.0) + jnp.log1p(jnp.exp(-jnp.abs(x)))


def _full(shape):
    nd = len(shape)
    return pl.BlockSpec(shape, lambda *_: (0,) * nd)


def _tile3(rows, width):
    return pl.BlockSpec((None, rows, width), lambda b, t: (b, t, 0))


def _inproj_kernel(n_blk, x_ref, g_ref, w_ref, qg_ref, kg_ref, e_ref,
                   za_ref, zb_ref, q_ref, k_ref, v_ref, gc_ref, kf_ref, vf_ref, hn_ref):
    rows = x_ref.shape[0]
    for s in range(n_blk):
        x = x_ref[:, s * D_MODEL:(s + 1) * D_MODEL]
        ms = jnp.mean(x * x, axis=-1, keepdims=True)
        hn_ref[s * rows:(s + 1) * rows, :] = (x * lax.rsqrt(ms + EPS) * g_ref[...]).astype(BF16)
    hn = hn_ref[...]

    def proj(cols):
        return jnp.dot(hn, w_ref[:, cols[0]:cols[1]], preferred_element_type=F32)

    def head_norm(z, gain):
        ss = jnp.dot((z * z).astype(BF16), e_ref[...], preferred_element_type=F32)
        return z * lax.rsqrt(ss * (1.0 / HEAD_DIM) + EPS) * gain

    def put(ref, z):
        c = z.shape[1]
        for s in range(n_blk):
            ref[:, s * c:(s + 1) * c] = z[s * rows:(s + 1) * rows, :].astype(ref.dtype)

    put(za_ref, proj(C_ZA))
    put(zb_ref, proj(C_ZB))
    q = head_norm(proj(C_Q), qg_ref[...]) * (HEAD_DIM ** -0.5)
    k = head_norm(proj(C_K), kg_ref[...])
    v = proj(C_V)
    put(q_ref, q)
    put(k_ref, k)
    put(v_ref, v)
    put(kf_ref, k)
    put(vf_ref, v)
    put(gc_ref, proj(C_GC))


def _inproj(x3, n_blk, rows_t, keep_rows, norm_g, w_in_bf, qg, kg, emat):
    nb, nrows, _ = x3.shape
    assert nrows % rows_t == 0 and keep_rows % rows_t == 0
    skip = (nrows - keep_rows) // rows_t
    kv_spec = pl.BlockSpec((None, rows_t, n_blk * D_ATT), lambda b, t: (b, jnp.maximum(t - skip, 0), 0))

    def out(c, dtype=F32):
        return jax.ShapeDtypeStruct((nb, nrows, n_blk * c), dtype)

    kv_shape = jax.ShapeDtypeStruct((nb, keep_rows, n_blk * D_ATT), F32)
    return pl.pallas_call(
        functools.partial(_inproj_kernel, n_blk),
        grid=(nb, nrows // rows_t),
        in_specs=[_tile3(rows_t, n_blk * D_MODEL), _full((1, D_MODEL)), _full((D_MODEL, D_IN)),
                  _full((1, D_ATT)), _full((1, D_ATT)), _full((D_ATT, D_ATT))],
        out_specs=[_tile3(rows_t, n_blk * 2 * D_LRU), _tile3(rows_t, n_blk * 2 * D_SSM),
                   _tile3(rows_t, n_blk * D_ATT), _tile3(rows_t, n_blk * D_ATT),
                   _tile3(rows_t, n_blk * D_ATT), _tile3(rows_t, n_blk * D_ATT), kv_spec, kv_spec],
        out_shape=[out(2 * D_LRU), out(2 * D_SSM), out(D_ATT, BF16), out(D_ATT, BF16),
                   out(D_ATT, BF16), out(D_ATT), kv_shape, kv_shape],
        scratch_shapes=[pltpu.VMEM((n_blk * rows_t, D_MODEL), BF16)],
        compiler_params=_cparams(2),
        name="inproj",
    )(x3, norm_g, w_in_bf, qg, kg, emat)


def _lru_conv(xa, prev, cw_ref, cb_ref, xc_ref, r_steps):
    for s in range(r_steps):
        acc = cb_ref[...] + cw_ref[3:4, :] * xa(s)
        for back in (1, 2, 3):
            src = xa(s - back) if s - back >= 0 else prev[back - s]
            acc = acc + cw_ref[3 - back:4 - back, :] * src
        xc_ref[s] = acc


def _lru_gates(xc, wg_ref, bg_ref, lam_ref):
    g = jnp.dot(xc.astype(BF16), wg_ref[...], preferred_element_type=F32) + bg_ref[...]
    r = _sigmoid(g[:, :D_LRU])
    i = _sigmoid(g[:, D_LRU:])
    log_a = -RG_C * r * _softplus(-lam_ref[...])
    a = jnp.exp(log_a)
    mult = jnp.sqrt(-jnp.tanh(log_a) * (a * a + 1.0))
    return a, mult, i


def _lru_prompt_kernel(r_steps, nr, za_ref, cw_ref, cb_ref, wg_ref, bg_ref, lam_ref,
                       ya_ref, conv_ref, hl_ref,
                       xc_ref, a_ref, b_ref, cin_ref, cx_ref, ch_ref):
    c = D_LRU
    ti = pl.program_id(1)
    last = pl.num_programs(1) - 1

    @pl.when(ti == 0)
    def _():
        cx_ref[...] = jnp.zeros_like(cx_ref)
        ch_ref[...] = jnp.zeros_like(ch_ref)

    def xa(s):
        return za_ref[:, s * 2 * c:s * 2 * c + c]

    def ga(s):
        return za_ref[:, s * 2 * c + c:(s + 1) * 2 * c]

    row = lax.broadcasted_iota(jnp.int32, (nr, c), 0)
    prev = {j: jnp.where(row == 0, cx_ref[j - 1:j, :], pltpu.roll(xa(r_steps - j), 1, 0))
            for j in (1, 2, 3)}
    _lru_conv(xa, prev, cw_ref, cb_ref, xc_ref, r_steps)

    xc = xc_ref[...].reshape(r_steps * nr, c)
    a, mult, gate_i = _lru_gates(xc, wg_ref, bg_ref, lam_ref)
    flat_row = lax.broadcasted_iota(jnp.int32, (r_steps * nr, c), 0)
    mult = jnp.where(jnp.logical_and(flat_row == 0, ti == 0), 1.0, mult)
    a_ref[...] = a.reshape(r_steps, nr, c)
    b_ref[...] = (mult * gate_i * xc).reshape(r_steps, nr, c)

    h = jnp.zeros((nr, c), F32)
    p = jnp.ones((nr, c), F32)
    for s in range(r_steps):
        a_s = a_ref[s]
        h = a_s * h + b_ref[s]
        p = a_s * p
        b_ref[s] = h
        a_ref[s] = p

    carry = ch_ref[0:1, :]
    for i in range(nr):
        cin_ref[i:i + 1, :] = carry
        carry = a_ref[r_steps - 1, i:i + 1, :] * carry + b_ref[r_steps - 1, i:i + 1, :]
    ch_ref[0:1, :] = carry

    cin = cin_ref[...]
    for s in range(r_steps):
        h_s = b_ref[s] + a_ref[s] * cin
        ya_ref[:, s * c:(s + 1) * c] = h_s * _silu(ga(s))

    for j in (1, 2, 3):
        cx_ref[j - 1:j, :] = xa(r_steps - j)[nr - 1:nr, :]

    @pl.when(ti == last)
    def _():
        hl_ref[...] = carry
        for j in range(CONV_W - 1):
            conv_ref[j:j + 1, :] = xa(r_steps - (CONV_W - 1) + j)[nr - 1:nr, :]


def _lru_prompt(za, cw, cb, wg, bg, lam):
    r, nr, c = CHUNK, CHUNK_ROWS, D_LRU
    nb, nrows, _ = za.shape
    assert nrows % nr == 0
    ya, conv_new, h_last = pl.pallas_call(
        functools.partial(_lru_prompt_kernel, r, nr),
        grid=(nb, nrows // nr),
        in_specs=[_tile3(nr, r * 2 * c),
                  _full((CONV_W, c)), _full((1, c)), _full((c, 2 * c)), _full((1, 2 * c)),
                  _full((1, c))],
        out_specs=[_tile3(nr, r * c),
                   pl.BlockSpec((None, CONV_W - 1, c), lambda b, t: (b, 0, 0)),
                   pl.BlockSpec((None, 1, c), lambda b, t: (b, 0, 0))],
        out_shape=[jax.ShapeDtypeStruct((nb, nrows, r * c), F32),
                   jax.ShapeDtypeStruct((nb, CONV_W - 1, c), F32),
                   jax.ShapeDtypeStruct((nb, 1, c), F32)],
        scratch_shapes=[pltpu.VMEM((r, nr, c), F32), pltpu.VMEM((r, nr, c), F32),
                        pltpu.VMEM((r, nr, c), F32), pltpu.VMEM((nr, c), F32),
                        pltpu.VMEM((8, c), F32), pltpu.VMEM((8, c), F32)],
        compiler_params=_cparams(2),
        name="lru_prompt",
    )(za, cw, cb, wg, bg, lam)
    return ya, conv_new, h_last.reshape(nb, c)


def _lru_sample_kernel(r_steps, za_ref, sc_ref, h0_ref, cw_ref, cb_ref, wg_ref, bg_ref, lam_ref,
                       ya_ref, conv_ref, hl_ref, xc_ref, a_ref, b_ref):
    c = D_LRU
    nr = za_ref.shape[0]

    def xa(s):
        return za_ref[:, s * 2 * c:s * 2 * c + c]

    def ga(s):
        return za_ref[:, s * 2 * c + c:(s + 1) * 2 * c]

    prev = {j: sc_ref[:, (CONV_W - 1 - j) * c:(CONV_W - j) * c] for j in (1, 2, 3)}
    _lru_conv(xa, prev, cw_ref, cb_ref, xc_ref, r_steps)
    xc = xc_ref[...].reshape(r_steps * nr, c)
    a, mult, gate_i = _lru_gates(xc, wg_ref, bg_ref, lam_ref)
    a_ref[...] = a.reshape(r_steps, nr, c)
    b_ref[...] = (mult * gate_i * xc).reshape(r_steps, nr, c)
    h = h0_ref[...]
    for s in range(r_steps):
        h = a_ref[s] * h + b_ref[s]
        ya_ref[:, s * c:(s + 1) * c] = h * _silu(ga(s))
    hl_ref[...] = h
    for j in range(CONV_W - 1):
        conv_ref[:, j * c:(j + 1) * c] = xa(r_steps - (CONV_W - 1) + j)


def _lru_sample(za, nb, seq_len, state_conv, h0, cw, cb, wg, bg, lam):
    r, c = seq_len, D_LRU
    zav = za.reshape(nb, r * 2 * c)
    scv = state_conv.reshape(nb, (CONV_W - 1) * c)
    ya, conv_new, h_last = pl.pallas_call(
        functools.partial(_lru_sample_kernel, r),
        grid=(1,),
        in_specs=[_full((nb, r * 2 * c)), _full((nb, (CONV_W - 1) * c)), _full((nb, c)),
                  _full((CONV_W, c)), _full((1, c)), _full((c, 2 * c)), _full((1, 2 * c)),
                  _full((1, c))],
        out_specs=[_full((nb, r * c)), _full((nb, (CONV_W - 1) * c)), _full((nb, c))],
        out_shape=[jax.ShapeDtypeStruct((nb, r * c), F32),
                   jax.ShapeDtypeStruct((nb, (CONV_W - 1) * c), F32),
                   jax.ShapeDtypeStruct((nb, c), F32)],
        scratch_shapes=[pltpu.VMEM((r, nb, c), F32), pltpu.VMEM((r, nb, c), F32),
                        pltpu.VMEM((r, nb, c), F32)],
        compiler_params=_cparams(1),
        name="lru_sample",
    )(zav, scv, h0, cw, cb, wg, bg, lam)
    return ya.reshape(nb * seq_len, c), conv_new.reshape(nb, CONV_W - 1, c), h_last


def _ssm_disc_kernel(r_steps, lr_ref, li_ref, ldt_ref, br_ref, bi_ref,
                     ab_ref, bbr_ref, bbi_ref, apow_ref):
    lr, li = lr_ref[...], li_ref[...]
    dt = jnp.exp(ldt_ref[...])
    mag = jnp.exp(lr * dt)
    ang = li * dt
    ab_re, ab_im = mag * jnp.cos(ang), mag * jnp.sin(ang)
    den = lr * lr + li * li
    xr, yi = ab_re - 1.0, ab_im
    coef_re = (xr * lr + yi * li) / den
    coef_im = (yi * lr - xr * li) / den
    br, bi = br_ref[...], bi_ref[...]
    bbr_ref[...] = coef_re * br - coef_im * bi
    bbi_ref[...] = coef_re * bi + coef_im * br
    ab_ref[0:1, :] = ab_re
    ab_ref[1:2, :] = ab_im
    pr, pi = ab_re, ab_im
    for s in range(r_steps):
        apow_ref[s:s + 1, 0:N_STATE] = pr
        apow_ref[s:s + 1, N_STATE:2 * N_STATE] = pi
        pr, pi = pr * ab_re - pi * ab_im, pr * ab_im + pi * ab_re


def _ssm_discretise(lam_re, lam_im, log_dt, b_re, b_im):
    def vec():
        return pl.BlockSpec((None, 1, N_STATE), lambda l: (l, 0, 0))

    def mat():
        return pl.BlockSpec((None, SSM_GROUP, N_STATE), lambda l: (l, 0, 0))

    return pl.pallas_call(
        functools.partial(_ssm_disc_kernel, CHUNK),
        grid=(DEPTH,),
        in_specs=[vec(), vec(), vec(), mat(), mat()],
        out_specs=[pl.BlockSpec((None, 2, N_STATE), lambda l: (l, 0, 0)), mat(), mat(),
                   pl.BlockSpec((None, CHUNK, 2 * N_STATE), lambda l: (l, 0, 0))],
        out_shape=[jax.ShapeDtypeStruct((DEPTH, 2, N_STATE), F32),
                   jax.ShapeDtypeStruct((DEPTH, SSM_GROUP, N_STATE), F32),
                   jax.ShapeDtypeStruct((DEPTH, SSM_GROUP, N_STATE), F32),
                   jax.ShapeDtypeStruct((DEPTH, CHUNK, 2 * N_STATE), F32)],
        compiler_params=_cparams(1),
        name="ssm_discretise",
    )(lam_re, lam_im, log_dt, b_re, b_im)


def _ssm_readout(x_all, u_all, g_all, cmat_ref, d_ref, gw_ref, gb_ref):
    y = jnp.dot(x_all, cmat_ref[...], preferred_element_type=F32) + d_ref[...] * u_all
    y = jax.nn.gelu(y)
    y = y * _sigmoid(jnp.dot(y.astype(BF16), gw_ref[...], preferred_element_type=F32) + gb_ref[...])
    return y * _silu(g_all)


def _ssm_prompt_kernel(r_steps, nr, zb_ref, bmat_ref, cmat_ref, ab_ref, apow_ref, d_ref,
                       gw_ref, gb_ref, yb_ref, sre_ref, sim_ref,
                       u_ref, g_ref, xs_ref, xb_ref, cin_ref, cst_ref):
    n, c = N_STATE, D_SSM
    ti = pl.program_id(1)
    last = pl.num_programs(1) - 1

    @pl.when(ti == 0)
    def _():
        cst_ref[...] = jnp.zeros_like(cst_ref)

    for s in range(r_steps):
        u_ref[s] = zb_ref[:, s * 2 * c:s * 2 * c + c]
        g_ref[s] = zb_ref[:, s * 2 * c + c:(s + 1) * 2 * c]
    u_all = u_ref[...].reshape(r_steps * nr, c)
    xs_ref[...] = jnp.dot(u_all.astype(BF16), bmat_ref[...],
                          preferred_element_type=F32).reshape(r_steps, nr, 2 * n)

    lane_block = 2 * LANES
    for lo in range(0, n, lane_block):
        hi = lo + lane_block
        ar, ai = ab_ref[0:1, lo:hi], ab_ref[1:2, lo:hi]
        xr = jnp.zeros((nr, lane_block), F32)
        xi = jnp.zeros((nr, lane_block), F32)
        for s in range(r_steps):
            nxr = ar * xr - ai * xi + xs_ref[s, :, lo:hi]
            nxi = ar * xi + ai * xr + xs_ref[s, :, n + lo:n + hi]
            xr, xi = nxr, nxi
            xs_ref[s, :, lo:hi] = xr
            xs_ref[s, :, n + lo:n + hi] = xi

    pr_end, pi_end = apow_ref[r_steps - 1:r_steps, 0:n], apow_ref[r_steps - 1:r_steps, n:2 * n]
    cr, ci = cst_ref[0:1, 0:n], cst_ref[0:1, n:2 * n]
    for i in range(nr):
        cin_ref[i:i + 1, 0:n] = cr
        cin_ref[i:i + 1, n:2 * n] = ci
        er = xs_ref[r_steps - 1, i:i + 1, 0:n]
        ei = xs_ref[r_steps - 1, i:i + 1, n:2 * n]
        cr, ci = pr_end * cr - pi_end * ci + er, pr_end * ci + pi_end * cr + ei
    cst_ref[0:1, 0:n] = cr
    cst_ref[0:1, n:2 * n] = ci

    for s in range(r_steps):
        pr, pi = apow_ref[s:s + 1, 0:n], apow_ref[s:s + 1, n:2 * n]
        cinr, cini = cin_ref[:, 0:n], cin_ref[:, n:2 * n]
        xb_ref[s, :, 0:n] = (xs_ref[s, :, 0:n] + pr * cinr - pi * cini).astype(BF16)
        xb_ref[s, :, n:2 * n] = (xs_ref[s, :, n:2 * n] + pr * cini + pi * cinr).astype(BF16)

    yb = _ssm_readout(xb_ref[...].reshape(r_steps * nr, 2 * n), u_all,
                      g_ref[...].reshape(r_steps * nr, c), cmat_ref, d_ref, gw_ref, gb_ref)
    for s in range(r_steps):
        yb_ref[:, s * c:(s + 1) * c] = yb[s * nr:(s + 1) * nr]

    @pl.when(ti == last)
    def _():
        sre_ref[...] = cr
        sim_ref[...] = ci


def _ssm_prompt(zb, bmat, cmat, ab, apow, d, gw, gb):
    r, nr, c, n = CHUNK, CHUNK_ROWS, D_SSM, N_STATE
    nb, nrows, _ = zb.shape
    yb, s_re, s_im = pl.pallas_call(
        functools.partial(_ssm_prompt_kernel, r, nr),
        grid=(nb, nrows // nr),
        in_specs=[_tile3(nr, r * 2 * c),
                  _full((c, 2 * n)), _full((2 * n, c)), _full((2, n)), _full((r, 2 * n)),
                  _full((1, c)), _full((c, c)), _full((1, c))],
        out_specs=[_tile3(nr, r * c),
                   pl.BlockSpec((None, 1, n), lambda b, t: (b, 0, 0)),
                   pl.BlockSpec((None, 1, n), lambda b, t: (b, 0, 0))],
        out_shape=[jax.ShapeDtypeStruct((nb, nrows, r * c), F32),
                   jax.ShapeDtypeStruct((nb, 1, n), F32),
                   jax.ShapeDtypeStruct((nb, 1, n), F32)],
        scratch_shapes=[pltpu.VMEM((r, nr, c), F32), pltpu.VMEM((r, nr, c), F32),
                        pltpu.VMEM((r, nr, 2 * n), F32), pltpu.VMEM((r, nr, 2 * n), BF16),
                        pltpu.VMEM((nr, 2 * n), F32), pltpu.VMEM((8, 2 * n), F32)],
        compiler_params=_cparams(2),
        name="ssm_prompt",
    )(zb, bmat, cmat, ab, apow, d, gw, gb)
    return yb, s_re.reshape(nb, n), s_im.reshape(nb, n)


def _ssm_sample_kernel(r_steps, zb_ref, s0r_ref, s0i_ref, bmat_ref, cmat_ref, ab_ref, d_ref,
                       gw_ref, gb_ref, yb_ref, sre_ref, sim_ref, u_ref, g_ref, xs_ref, xb_ref):
    n, c = N_STATE, D_SSM
    nr = zb_ref.shape[0]
    for s in range(r_steps):
        u_ref[s] = zb_ref[:, s * 2 * c:s * 2 * c + c]
        g_ref[s] = zb_ref[:, s * 2 * c + c:(s + 1) * 2 * c]
    u_all = u_ref[...].reshape(r_steps * nr, c)
    xs_ref[...] = jnp.dot(u_all.astype(BF16), bmat_ref[...],
                          preferred_element_type=F32).reshape(r_steps, nr, 2 * n)
    for lo in range(0, n, LANES):
        hi = lo + LANES
        ar, ai = ab_ref[0:1, lo:hi], ab_ref[1:2, lo:hi]
        xr, xi = s0r_ref[:, lo:hi], s0i_ref[:, lo:hi]
        for s in range(r_steps):
            nxr = ar * xr - ai * xi + xs_ref[s, :, lo:hi]
            nxi = ar * xi + ai * xr + xs_ref[s, :, n + lo:n + hi]
            xr, xi = nxr, nxi
            xb_ref[s, :, lo:hi] = xr.astype(BF16)
            xb_ref[s, :, n + lo:n + hi] = xi.astype(BF16)
        sre_ref[:, lo:hi] = xr
        sim_ref[:, lo:hi] = xi
    yb = _ssm_readout(xb_ref[...].reshape(r_steps * nr, 2 * n), u_all,
                      g_ref[...].reshape(r_steps * nr, c), cmat_ref, d_ref, gw_ref, gb_ref)
    for s in range(r_steps):
        yb_ref[:, s * c:(s + 1) * c] = yb[s * nr:(s + 1) * nr]


def _ssm_sample(zb, nb, seq_len, s0_re, s0_im, bmat, cmat, ab, d, gw, gb):
    r, c, n = seq_len, D_SSM, N_STATE
    zbv = zb.reshape(nb, r * 2 * c)
    yb, s_re, s_im = pl.pallas_call(
        functools.partial(_ssm_sample_kernel, r),
        grid=(1,),
        in_specs=[_full((nb, r * 2 * c)), _full((nb, n)), _full((nb, n)),
                  _full((c, 2 * n)), _full((2 * n, c)), _full((2, n)),
                  _full((1, c)), _full((c, c)), _full((1, c))],
        out_specs=[_full((nb, r * c)), _full((nb, n)), _full((nb, n))],
        out_shape=[jax.ShapeDtypeStruct((nb, r * c), F32),
                   jax.ShapeDtypeStruct((nb, n), F32),
                   jax.ShapeDtypeStruct((nb, n), F32)],
        scratch_shapes=[pltpu.VMEM((r, nb, c), F32), pltpu.VMEM((r, nb, c), F32),
                        pltpu.VMEM((r, nb, 2 * n), F32), pltpu.VMEM((r, nb, 2 * n), BF16)],
        compiler_params=_cparams(1),
        name="ssm_sample",
    )(zbv, s0_re, s0_im, bmat, cmat, ab, d, gw, gb)
    return yb.reshape(nb * seq_len, c), s_re, s_im


def _attend(q, kk, vv, bias_ref, pen):
    blk = q.shape[0]
    lane = lax.broadcasted_iota(jnp.int32, (1, LANES), 1)
    outs, lses = [], []
    for pair in range(N_HEADS // 2):
        sl = slice(pair * LANES, (pair + 1) * LANES)
        qp, kp, vp = q[:, sl], kk[:, sl], vv[:, sl]
        acc = jnp.zeros((blk, LANES), F32)
        lse = jnp.zeros((blk, LANES), F32)
        for hh in range(2):
            head_lanes = (lane >= HEAD_DIM) if hh else (lane < HEAD_DIM)
            qh = jnp.where(head_lanes, qp, jnp.zeros_like(qp))
            s = lax.dot_general(qh, kp, (((1,), (1,)), ((), ())), preferred_element_type=F32)
            s = s + bias_ref[2 * pair + hh]
            if pen is not None:
                s = s + pen
            m = jnp.max(s, axis=-1, keepdims=True)
            e = jnp.exp(s - m)
            den = jnp.sum(e, axis=-1, keepdims=True)
            vh = jnp.where(head_lanes, vp, jnp.zeros_like(vp))
            pv = jnp.dot(e.astype(BF16), vh, preferred_element_type=F32)
            acc = acc + pv * (1.0 / den)
            lse = jnp.where(head_lanes, m + jnp.log(den), lse)
        outs.append(acc)
        lses.append(lse)
    return outs, lses


def _first_block_penalty(is_first):
    col = lax.broadcasted_iota(jnp.int32, (1, 2 * ATT_BLOCK), 1)
    return jnp.where(jnp.logical_and(col < ATT_BLOCK, is_first), NEG_INF, 0.0).astype(F32)


def _attn_d16_kernel(q_ref, k_ref, v_ref, bias_ref, o_ref, l_ref, kk_ref, vv_ref):
    blk = ATT_BLOCK
    jb = pl.program_id(2)

    @pl.when(jb == 0)
    def _():
        kk_ref[0:blk, :] = jnp.zeros((blk, D_ATT), BF16)
        vv_ref[0:blk, :] = jnp.zeros((blk, D_ATT), BF16)

    kk_ref[blk:2 * blk, :] = k_ref[...]
    vv_ref[blk:2 * blk, :] = v_ref[...]
    outs, lses = _attend(q_ref, kk_ref, vv_ref, bias_ref, _first_block_penalty(jb == 0))
    for pair in range(N_HEADS // 2):
        sl = slice(pair * LANES, (pair + 1) * LANES)
        o_ref[:, sl] = outs[pair]
        l_ref[:, sl] = lses[pair]
    kk_ref[0:blk, :] = k_ref[...]
    vv_ref[0:blk, :] = v_ref[...]


def _attn_d4_kernel(q_ref, k_ref, v_ref, bias_ref, o_ref, l_ref, kp_ref, vp_ref, qq_ref, kk_ref, vv_ref):
    blk, rows, c = ATT_BLOCK, q_ref.shape[0], D_ATT
    n_sub = CHUNK // 4
    jb = pl.program_id(1)

    @pl.when(jb == 0)
    def _():
        kp_ref[...] = jnp.zeros_like(kp_ref)
        vp_ref[...] = jnp.zeros_like(vp_ref)

    pen = _first_block_penalty(jb == 0)
    for r in range(4):
        for m in range(n_sub):
            sl = slice((r + 4 * m) * c, (r + 4 * m + 1) * c)
            dst = slice(m * rows, (m + 1) * rows)
            dst2 = slice(blk + m * rows, blk + (m + 1) * rows)
            qq_ref[dst, :] = q_ref[:, sl]
            kk_ref[dst, :] = kp_ref[:, sl]
            kk_ref[dst2, :] = k_ref[:, sl]
            vv_ref[dst, :] = vp_ref[:, sl]
            vv_ref[dst2, :] = v_ref[:, sl]
        outs, lses = _attend(qq_ref, kk_ref, vv_ref, bias_ref, pen)
        for pair in range(N_HEADS // 2):
            for m in range(n_sub):
                lo = (r + 4 * m) * c + pair * LANES
                o_ref[:, lo:lo + LANES] = outs[pair][m * rows:(m + 1) * rows]
                l_ref[:, lo:lo + LANES] = lses[pair][m * rows:(m + 1) * rows]
    kp_ref[...] = k_ref[...]
    vp_ref[...] = v_ref[...]


def _attn_d1_kernel(q_ref, k_ref, v_ref, bias_ref, o_ref, l_ref,
                    qf_ref, kf_ref, vf_ref, qq_ref, kk_ref, vv_ref):
    blk, c = ATT_BLOCK, D_ATT
    half = q_ref.shape[0] // 2
    jb = pl.program_id(1)

    @pl.when(jb == 0)
    def _():
        kk_ref[0:blk, :] = jnp.zeros((blk, c), BF16)
        vv_ref[0:blk, :] = jnp.zeros((blk, c), BF16)

    for s in range(CHUNK):
        sl = slice(s * c, (s + 1) * c)
        for src, dst in ((q_ref, qf_ref), (k_ref, kf_ref), (v_ref, vf_ref)):
            piece = src[:, sl].astype(F32)
            dst[s * half:(s + 1) * half, :] = piece[0:half]
            dst[blk + s * half:blk + (s + 1) * half, :] = piece[half:2 * half]
    kk_ref[blk:3 * blk, :] = kf_ref[...].astype(BF16)
    vv_ref[blk:3 * blk, :] = vf_ref[...].astype(BF16)
    qq_ref[...] = qf_ref[...].astype(BF16)
    for part in range(2):
        pen = _first_block_penalty(jb == 0) if part == 0 else None
        outs, lses = _attend(qq_ref.at[pl.ds(part * blk, blk), :], kk_ref.at[pl.ds(part * blk, 2 * blk), :],
                             vv_ref.at[pl.ds(part * blk, 2 * blk), :], bias_ref, pen)
        rows = slice(part * half, (part + 1) * half)
        for pair in range(N_HEADS // 2):
            for s in range(CHUNK):
                lo = s * c + pair * LANES
                o_ref[rows, lo:lo + LANES] = outs[pair][s * half:(s + 1) * half]
                l_ref[rows, lo:lo + LANES] = lses[pair][s * half:(s + 1) * half]
    for lo in range(0, c, LANES):
        kk_ref[0:blk, lo:lo + LANES] = kk_ref[2 * blk:3 * blk, lo:lo + LANES]
        vv_ref[0:blk, lo:lo + LANES] = vv_ref[2 * blk:3 * blk, lo:lo + LANES]


def _prompt_bias(order, dil, slopes):
    dist = order[:, None] - jnp.concatenate([order - ATT_BLOCK, order])[None, :]
    valid = (dist >= 0) & (dist <= ATT_J)
    bias = -slopes[:, None, None] * (dist * dil).astype(F32)[None]
    return jnp.where(valid[None], bias, NEG_INF).astype(F32)


def _attn_prompt(q, k, v, dil, slopes):
    blk, c = ATT_BLOCK, D_ATT
    nb, nrows, width = q.shape
    a = jnp.arange(blk)
    out_shape = [jax.ShapeDtypeStruct((nb, nrows, width), F32)] * 2
    bias_spec = _full((N_HEADS, blk, 2 * blk))
    kv_scratch = [pltpu.VMEM((2 * blk, c), BF16), pltpu.VMEM((2 * blk, c), BF16)]
    if dil == 16:
        spec = pl.BlockSpec((None, blk, c), lambda b, r, j: (b, j, r))
        return pl.pallas_call(
            _attn_d16_kernel, grid=(nb, CHUNK, nrows // blk),
            in_specs=[spec, spec, spec, bias_spec], out_specs=[spec, spec], out_shape=out_shape,
            scratch_shapes=kv_scratch, compiler_params=_cparams(3), name="attn_prompt_d16",
        )(q, k, v, _prompt_bias(a, dil, slopes))
    if dil == 4:
        rows = blk // (CHUNK // 4)
        spec = _tile3(rows, width)
        order = 4 * (a % rows) + a // rows
        return pl.pallas_call(
            _attn_d4_kernel, grid=(nb, nrows // rows),
            in_specs=[spec, spec, spec, bias_spec], out_specs=[spec, spec], out_shape=out_shape,
            scratch_shapes=[pltpu.VMEM((rows, width), BF16), pltpu.VMEM((rows, width), BF16),
                            pltpu.VMEM((blk, c), BF16)] + kv_scratch,
            compiler_params=_cparams(2), name="attn_prompt_d4",
        )(q, k, v, _prompt_bias(order, dil, slopes))
    assert dil == 1
    rows = 2 * blk // CHUNK
    spec = _tile3(rows, width)
    order = CHUNK * (a % (rows // 2)) + a // (rows // 2)
    return pl.pallas_call(
        _attn_d1_kernel, grid=(nb, nrows // rows),
        in_specs=[spec, spec, spec, bias_spec], out_specs=[spec, spec], out_shape=out_shape,
        scratch_shapes=[pltpu.VMEM((2 * blk, c), F32)] * 3
                       + [pltpu.VMEM((2 * blk, c), BF16), pltpu.VMEM((3 * blk, c), BF16),
                          pltpu.VMEM((3 * blk, c), BF16)],
        compiler_params=_cparams(2), name="attn_prompt_d1",
    )(q, k, v, _prompt_bias(order, dil, slopes))


def _attn_sample_kernel(bs, t_new, q_ref, kn_ref, vn_ref, kt_ref, vt_ref, bias_ref, o_ref):
    rows_qh = N_HEADS * t_new
    row_head = lax.broadcasted_iota(jnp.int32, (rows_qh, D_ATT), 0) // t_new
    lane_head = lax.broadcasted_iota(jnp.int32, (rows_qh, D_ATT), 1) // HEAD_DIM
    own = row_head == lane_head
    pad = jnp.zeros((LANES - t_new, D_ATT), F32)
    nt = (((1,), (1,)), ((), ()))
    for b in range(bs):
        kt = kt_ref[b].reshape(D_ATT, PAST_LEN).astype(BF16)
        vt = vt_ref[b].reshape(D_ATT, PAST_LEN).astype(BF16)
        kn = jnp.concatenate([kn_ref[b], pad], axis=0).astype(BF16)
        vn = jnp.concatenate([vn_ref[b], pad], axis=0).astype(BF16)
        q8 = q_ref[b].astype(F32)
        qbd = jnp.where(own, jnp.concatenate([q8] * N_HEADS, axis=0), 0.0).astype(BF16)
        s = jnp.concatenate([jnp.dot(qbd, kt, preferred_element_type=F32),
                             lax.dot_general(qbd, kn, nt, preferred_element_type=F32)], axis=1)
        es, lses = [], []
        for p in range(len(DILATED)):
            sp = s + bias_ref[p]
            m = jnp.max(sp, axis=-1, keepdims=True)
            e = jnp.exp(sp - m)
            den = jnp.sum(e, axis=-1, keepdims=True)
            es.append(e / den)
            lses.append(m + jnp.log(den))
        top = jnp.maximum(jnp.maximum(lses[0], lses[1]), lses[2])
        ws = [jnp.exp(l - top) for l in lses]
        wsum = ws[0] + ws[1] + ws[2]
        pc = ((ws[0] * es[0] + ws[1] * es[1] + ws[2] * es[2]) / wsum).astype(BF16)
        o_all = (lax.dot_general(pc[:, :PAST_LEN], vt, nt, preferred_element_type=F32)
                 + jnp.dot(pc[:, PAST_LEN:], vn, preferred_element_type=F32))
        o_all = jnp.where(own, o_all, 0.0)
        out = o_all[0:t_new]
        for h in range(1, N_HEADS):
            out = out + o_all[h * t_new:(h + 1) * t_new]
        o_ref[b] = out


def _sample_bias(t_new, slopes):
    col = jnp.arange(PAST_LEN + LANES)
    real = col < PAST_LEN + t_new
    t = jnp.arange(t_new)
    dist = PAST_LEN + t[:, None] - col[None, :]
    out = []
    for window, dil in DILATED:
        valid = real[None, :] & (dist >= 0) & (dist % dil == 0) & (dist <= window)
        bias = -slopes[:, None, None] * dist.astype(F32)[None]
        bias = jnp.where(valid[None], bias, NEG_INF)
        out.append(bias.reshape(N_HEADS * t_new, PAST_LEN + LANES))
    return jnp.stack(out, axis=0).astype(F32)


def _attn_sample(q, k_new, v_new, cache_kt, cache_vt, layer, nb, t_new, slopes):
    bs = 2
    assert nb % bs == 0 and cache_kt.shape[-1] == PAST_LEN
    spec_c = pl.BlockSpec((None, bs, N_HEADS, HEAD_DIM, PAST_LEN), lambda i: (layer, i, 0, 0, 0))
    spec_new = pl.BlockSpec((bs, t_new, D_ATT), lambda i: (i, 0, 0))
    o = pl.pallas_call(
        functools.partial(_attn_sample_kernel, bs, t_new),
        grid=(nb // bs,),
        in_specs=[spec_new, spec_new, spec_new, spec_c, spec_c,
                  _full((len(DILATED), N_HEADS * t_new, PAST_LEN + LANES))],
        out_specs=spec_new,
        out_shape=jax.ShapeDtypeStruct((nb, t_new, D_ATT), F32),
        compiler_params=_cparams(1),
        name="attn_sample",
    )(q.reshape(nb, t_new, D_ATT), k_new.reshape(nb, t_new, D_ATT), v_new.reshape(nb, t_new, D_ATT),
      cache_kt, cache_vt, _sample_bias(t_new, slopes))
    return o.reshape(nb * t_new, D_ATT)


def _outproj_kernel(n_blk, n_pat, x_ref, ya_ref, yb_ref, gc_ref, *refs):
    n_lse = n_pat if n_pat > 1 else 0
    o_refs, l_refs = refs[:n_pat], refs[n_pat:n_pat + n_lse]
    og_ref, w_ref, out_ref, yn_ref = refs[n_pat + n_lse:]
    rows = x_ref.shape[0]
    a_hi, b_hi = D_LRU, D_LRU + D_SSM

    def norm_into(y, lo, hi, s):
        yn = y * lax.rsqrt(jnp.mean(y * y, axis=-1, keepdims=True) + EPS) * og_ref[:, lo:hi]
        yn_ref[s * rows:(s + 1) * rows, lo:hi] = yn.astype(BF16)

    for s in range(n_blk):
        att = slice(s * D_ATT, (s + 1) * D_ATT)
        if n_pat == 1:
            o = o_refs[0][:, att]
        else:
            lses = [r[:, att] for r in l_refs]
            top = functools.reduce(jnp.maximum, lses)
            ws = [jnp.exp(l - top) for l in lses]
            o = functools.reduce(lambda x, y: x + y, [w * r[:, att] for w, r in zip(ws, o_refs)])
            o = o / functools.reduce(lambda x, y: x + y, ws)
        yc = o * _silu(gc_ref[:, att])
        norm_into(ya_ref[:, s * D_LRU:(s + 1) * D_LRU], 0, a_hi, s)
        norm_into(yb_ref[:, s * D_SSM:(s + 1) * D_SSM], a_hi, b_hi, s)
        norm_into(yc, b_hi, D_MODEL, s)
    res = jnp.dot(yn_ref[...], w_ref[...], preferred_element_type=F32)
    for s in range(n_blk):
        sl = slice(s * D_MODEL, (s + 1) * D_MODEL)
        out_ref[:, sl] = x_ref[:, sl] + res[s * rows:(s + 1) * rows]


def _outproj(x3, n_blk, rows_t, ya, yb, gc, outs, lses, og, w_out_bf):
    nb, nrows, _ = x3.shape
    n_pat = len(outs)
    extra = list(outs) + list(lses)
    att_spec = _tile3(rows_t, n_blk * D_ATT)
    return pl.pallas_call(
        functools.partial(_outproj_kernel, n_blk, n_pat),
        grid=(nb, nrows // rows_t),
        in_specs=[_tile3(rows_t, n_blk * D_MODEL), _tile3(rows_t, n_blk * D_LRU),
                  _tile3(rows_t, n_blk * D_SSM), att_spec] + [att_spec] * len(extra)
                 + [_full((1, D_MODEL)), _full((D_MODEL, D_MODEL))],
        out_specs=_tile3(rows_t, n_blk * D_MODEL),
        out_shape=jax.ShapeDtypeStruct(x3.shape, F32),
        scratch_shapes=[pltpu.VMEM((n_blk * rows_t, D_MODEL), BF16)],
        compiler_params=_cparams(2),
        name="outproj",
    )(x3, ya, yb, gc, *extra, og, w_out_bf)


def _block_diag(blocks):
    k, i, j = blocks.shape
    eye = jnp.eye(k, dtype=blocks.dtype)
    return jnp.einsum("kij,kl->kilj", blocks, eye).reshape(k * i, k * j)


def kernel(x_prompt, x_sample, state_conv, state_lru, state_ssm_re, state_ssm_im, cache_k, cache_v,
           norm_g, w_in, conv_w, conv_b, w_r, b_r, w_i, b_i, lru_lambda,
           ssm_lambda_re, ssm_lambda_im, ssm_log_dt, ssm_b_re, ssm_b_im, ssm_c_re, ssm_c_im,
           ssm_d, glu_w, glu_b, q_norm_g, k_norm_g, out_norm_g, w_out):
    bp, seq, _ = x_prompt.shape
    bs_, t_new, _ = x_sample.shape
    keep = min(PAST_LEN, seq)
    assert seq % (CHUNK * CHUNK_ROWS) == 0 and keep % (CHUNK * CHUNK_ROWS) == 0
    n_s = bs_ * t_new
    rows_s = min(ROW_TILE, n_s)
    xp = x_prompt.reshape(bp, seq // CHUNK, CHUNK * D_MODEL)
    xs = x_sample.reshape(1, n_s, D_MODEL)
    cache_kt = jnp.transpose(cache_k, (0, 1, 3, 4, 2))
    cache_vt = jnp.transpose(cache_v, (0, 1, 3, 4, 2))

    h = jnp.arange(1, N_HEADS + 1, dtype=F32)
    slopes = jnp.exp2(-8.0 * h / N_HEADS)
    head_of = jnp.arange(D_ATT) // HEAD_DIM
    emat = (head_of[:, None] == head_of[None, :]).astype(BF16)

    def chan_major(b):
        return jnp.transpose(b, (0, 3, 1, 2)).reshape(DEPTH, SSM_GROUP, N_STATE)

    def vec(a):
        return a.reshape(DEPTH, 1, N_STATE)

    ab_all, bbr_all, bbi_all, apow_all = _ssm_discretise(
        vec(ssm_lambda_re), vec(ssm_lambda_im), vec(ssm_log_dt), chan_major(ssm_b_re), chan_major(ssm_b_im))

    sp_list, ss_list = [], []
    for l in range(DEPTH):
        w_in_bf = w_in[l].astype(BF16)
        w_out_bf = w_out[l].astype(BF16)
        ng = norm_g[l].reshape(1, D_MODEL)
        qg = jnp.tile(q_norm_g[l], N_HEADS).reshape(1, D_ATT)
        kg = jnp.tile(k_norm_g[l], N_HEADS).reshape(1, D_ATT)
        og = out_norm_g[l].reshape(1, D_MODEL)
        cw, cb = conv_w[l], conv_b[l].reshape(1, D_LRU)
        wg = jnp.concatenate([_block_diag(w_r[l]), _block_diag(w_i[l])], axis=1).astype(BF16)
        bg = jnp.concatenate([b_r[l], b_i[l]]).reshape(1, 2 * D_LRU)
        lam = lru_lambda[l].reshape(1, D_LRU)

        def b_matrix(bb):
            blocks = jnp.transpose(bb.reshape(SSM_GROUP, SSM_GROUPS, SSM_STATE), (1, 0, 2))
            return _block_diag(blocks)
        bmat = jnp.concatenate([b_matrix(bbr_all[l]), b_matrix(bbi_all[l])], axis=1).astype(BF16)
        cmat = jnp.concatenate([_block_diag(jnp.transpose(ssm_c_re[l], (0, 2, 1))),
                                -_block_diag(jnp.transpose(ssm_c_im[l], (0, 2, 1)))], axis=0).astype(BF16)
        d = ssm_d[l].reshape(1, D_SSM)
        gw = glu_w[l].astype(BF16)
        gb = glu_b[l].reshape(1, D_SSM)

        za, zb, q, k, v, gc, k_rows, v_rows = _inproj(xp, CHUNK, CHUNK_ROWS, keep // CHUNK,
                                                      ng, w_in_bf, qg, kg, emat)
        ya, conv_new, h_last = _lru_prompt(za, cw, cb, wg, bg, lam)
        yb, s_re, s_im = _ssm_prompt(zb, bmat, cmat, ab_all[l], apow_all[l], d, gw, gb)
        pats = [_attn_prompt(q, k, v, dil, slopes) for _, dil in DILATED]
        xp = _outproj(xp, CHUNK, CHUNK_ROWS, ya, yb, gc, [o for o, _ in pats], [s for _, s in pats],
                      og, w_out_bf)
        sp_list.append((conv_new, h_last,
                        s_re.reshape(bp, SSM_GROUPS, SSM_STATE), s_im.reshape(bp, SSM_GROUPS, SSM_STATE),
                        k_rows.reshape(bp, keep, N_HEADS, HEAD_DIM), v_rows.reshape(bp, keep, N_HEADS, HEAD_DIM)))

        za, zb, q, k, v, gc, k_rows, v_rows = _inproj(xs, 1, rows_s, n_s, ng, w_in_bf, qg, kg, emat)
        ya, conv_new, h_last = _lru_sample(za, bs_, t_new, state_conv[l], state_lru[l], cw, cb, wg, bg, lam)
        yb, s_re, s_im = _ssm_sample(zb, bs_, t_new, state_ssm_re[l].reshape(bs_, N_STATE),
                                     state_ssm_im[l].reshape(bs_, N_STATE), bmat, cmat, ab_all[l], d, gw, gb)
        o = _attn_sample(q, k_rows, v_rows, cache_kt, cache_vt, l, bs_, t_new, slopes)
        xs = _outproj(xs, 1, rows_s, ya.reshape(1, n_s, D_LRU), yb.reshape(1, n_s, D_SSM), gc,
                      [o.reshape(1, n_s, D_ATT)], [], og, w_out_bf)
        ss_list.append((conv_new, h_last,
                        s_re.reshape(bs_, SSM_GROUPS, SSM_STATE), s_im.reshape(bs_, SSM_GROUPS, SSM_STATE),
                        k_rows.reshape(bs_, t_new, N_HEADS, HEAD_DIM), v_rows.reshape(bs_, t_new, N_HEADS, HEAD_DIM)))

    def stack(lst, i):
        return jnp.stack([s[i] for s in lst], axis=0)

    return (xp.reshape(bp, seq, D_MODEL), xs.reshape(bs_, t_new, D_MODEL),
            stack(sp_list, 0), stack(sp_list, 1), stack(sp_list, 2), stack(sp_list, 3),
            stack(sp_list, 4), stack(sp_list, 5),
            stack(ss_list, 0), stack(ss_list, 1), stack(ss_list, 2), stack(ss_list, 3),
            stack(ss_list, 4), stack(ss_list, 5))
```

```python
import functools

import numpy as np
import jax
import jax.numpy as jnp
from jax import lax
from jax.experimental import pallas as pl
from jax.experimental.pallas import tpu as pltpu

F32 = jnp.float32
BF16 = jnp.bfloat16

D_MODEL = 1024
DEPTH = 4
PAST_LEN = 2048
D_LRU = 384
CONV_W = 4
RG_C = 8.0
D_SSM = 256
SSM_GROUP = 16
SSM_GROUPS = 16
SSM_STATE = 64
N_STATE = SSM_GROUPS * SSM_STATE
D_ATT = 384
HEAD_DIM = 64
N_HEADS = 6
DILATED = ((128, 1), (512, 4), (2048, 16))
ATT_J = 128
D_IN = 2 * D_LRU + 2 * D_SSM + 4 * D_ATT
EPS = 1e-6
NEG_INF = -1e30
LOG2_E = float(np.log2(np.e))
ALIBI_SLOPES = np.exp2(-8.0 * np.arange(1, N_HEADS + 1, dtype=np.float32) / N_HEADS).astype(np.float32)

C_ZA = (0, 2 * D_LRU)
C_ZB = (2 * D_LRU, 2 * D_LRU + 2 * D_SSM)
C_Q = (C_ZB[1], C_ZB[1] + D_ATT)
C_K = (C_Q[1], C_Q[1] + D_ATT)
C_V = (C_K[1], C_K[1] + D_ATT)
C_GC = (C_V[1], C_V[1] + D_ATT)

V7X_VMEM_LIMIT = 56 * 1024 * 1024
LANES = 128
CHUNK = 16
CHUNK_ROWS = 32
ROW_TILE = 512
ATT_BLOCK = ATT_J


def _cparams(n_axes):
    return pltpu.CompilerParams(dimension_semantics=("arbitrary",) * n_axes,
                                vmem_limit_bytes=V7X_VMEM_LIMIT)


def _sigmoid(x):
    return 1.0 / (1.0 + jnp.exp(-x))


def _silu(x):
    return x * _sigmoid(x)


def _softplus(x):
    return jnp.maximum(x, 0.0) + jnp.log1p(jnp.exp(-jnp.abs(x)))


def _full(shape):
    nd = len(shape)
    return pl.BlockSpec(shape, lambda *_: (0,) * nd)


def _tile3(rows, width):
    return pl.BlockSpec((None, rows, width), lambda b, t: (b, t, 0))


def _inproj_kernel(n_blk, x_ref, g_ref, w_ref, qg_ref, kg_ref, e_ref,
                   za_ref, zb_ref, q_ref, k_ref, v_ref, gc_ref, kf_ref, vf_ref, hn_ref):
    rows = x_ref.shape[0]
    for s in range(n_blk):
        x = x_ref[:, s * D_MODEL:(s + 1) * D_MODEL]
        ms = jnp.mean(x * x, axis=-1, keepdims=True)
        hn_ref[s * rows:(s + 1) * rows, :] = (x * lax.rsqrt(ms + EPS) * g_ref[...]).astype(BF16)
    hn = hn_ref[...]

    def proj(cols):
        return jnp.dot(hn, w_ref[:, cols[0]:cols[1]], preferred_element_type=F32)

    def head_norm(z, gain):
        ss = jnp.dot((z * z).astype(BF16), e_ref[...], preferred_element_type=F32)
        return z * lax.rsqrt(ss * (1.0 / HEAD_DIM) + EPS) * gain

    def put(ref, z):
        c = z.shape[1]
        for s in range(n_blk):
            ref[:, s * c:(s + 1) * c] = z[s * rows:(s + 1) * rows, :].astype(ref.dtype)

    put(za_ref, proj(C_ZA))
    put(zb_ref, proj(C_ZB))
    q = head_norm(proj(C_Q), qg_ref[...]) * (HEAD_DIM ** -0.5 * LOG2_E)
    k = head_norm(proj(C_K), kg_ref[...])
    v = proj(C_V)
    put(q_ref, q)
    put(k_ref, k)
    put(v_ref, v)
    put(kf_ref, k)
    put(vf_ref, v)
    put(gc_ref, proj(C_GC))


def _inproj(x3, n_blk, rows_t, keep_rows, norm_g, w_in_bf, qg, kg, emat):
    nb, nrows, _ = x3.shape
    assert nrows % rows_t == 0 and keep_rows % rows_t == 0
    skip = (nrows - keep_rows) // rows_t
    kv_spec = pl.BlockSpec((None, rows_t, n_blk * D_ATT), lambda b, t: (b, jnp.maximum(t - skip, 0), 0))

    def out(c, dtype=F32):
        return jax.ShapeDtypeStruct((nb, nrows, n_blk * c), dtype)

    kv_shape = jax.ShapeDtypeStruct((nb, keep_rows, n_blk * D_ATT), F32)
    return pl.pallas_call(
        functools.partial(_inproj_kernel, n_blk),
        grid=(nb, nrows // rows_t),
        in_specs=[_tile3(rows_t, n_blk * D_MODEL), _full((1, D_MODEL)), _full((D_MODEL, D_IN)),
                  _full((1, D_ATT)), _full((1, D_ATT)), _full((D_ATT, D_ATT))],
        out_specs=[_tile3(rows_t, n_blk * 2 * D_LRU), _tile3(rows_t, n_blk * 2 * D_SSM),
                   _tile3(rows_t, n_blk * D_ATT), _tile3(rows_t, n_blk * D_ATT),
                   _tile3(rows_t, n_blk * D_ATT), _tile3(rows_t, n_blk * D_ATT), kv_spec, kv_spec],
        out_shape=[out(2 * D_LRU), out(2 * D_SSM), out(D_ATT, BF16), out(D_ATT, BF16),
                   out(D_ATT, BF16), out(D_ATT), kv_shape, kv_shape],
        scratch_shapes=[pltpu.VMEM((n_blk * rows_t, D_MODEL), BF16)],
        compiler_params=_cparams(2),
        name="inproj",
    )(x3, norm_g, w_in_bf, qg, kg, emat)


def _lru_conv(xa, prev, cw_ref, cb_ref, xc_ref, r_steps):
    for s in range(r_steps):
        acc = cb_ref[...] + cw_ref[3:4, :] * xa(s)
        for back in (1, 2, 3):
            src = xa(s - back) if s - back >= 0 else prev[back - s]
            acc = acc + cw_ref[3 - back:4 - back, :] * src
        xc_ref[s] = acc


def _lru_gates(xc, wg_ref, bg_ref, lam_ref):
    g = jnp.dot(xc.astype(BF16), wg_ref[...], preferred_element_type=F32) + bg_ref[...]
    r = _sigmoid(g[:, :D_LRU])
    i = _sigmoid(g[:, D_LRU:])
    log_a = -RG_C * r * _softplus(-lam_ref[...])
    a = jnp.exp(log_a)
    mult = jnp.sqrt(-jnp.tanh(log_a) * (a * a + 1.0))
    return a, mult, i


def _lru_prompt_kernel(r_steps, nr, za_ref, cw_ref, cb_ref, wg_ref, bg_ref, lam_ref,
                       ya_ref, conv_ref, hl_ref,
                       xc_ref, a_ref, b_ref, cin_ref, cx_ref, ch_ref):
    c = D_LRU
    ti = pl.program_id(1)
    last = pl.num_programs(1) - 1

    @pl.when(ti == 0)
    def _():
        cx_ref[...] = jnp.zeros_like(cx_ref)
        ch_ref[...] = jnp.zeros_like(ch_ref)

    def xa(s):
        return za_ref[:, s * 2 * c:s * 2 * c + c]

    def ga(s):
        return za_ref[:, s * 2 * c + c:(s + 1) * 2 * c]

    row = lax.broadcasted_iota(jnp.int32, (nr, c), 0)
    prev = {j: jnp.where(row == 0, cx_ref[j - 1:j, :], pltpu.roll(xa(r_steps - j), 1, 0))
            for j in (1, 2, 3)}
    _lru_conv(xa, prev, cw_ref, cb_ref, xc_ref, r_steps)

    xc = xc_ref[...].reshape(r_steps * nr, c)
    a, mult, gate_i = _lru_gates(xc, wg_ref, bg_ref, lam_ref)
    flat_row = lax.broadcasted_iota(jnp.int32, (r_steps * nr, c), 0)
    mult = jnp.where(jnp.logical_and(flat_row == 0, ti == 0), 1.0, mult)
    a_ref[...] = a.reshape(r_steps, nr, c)
    b_ref[...] = (mult * gate_i * xc).reshape(r_steps, nr, c)

    h = jnp.zeros((nr, c), F32)
    p = jnp.ones((nr, c), F32)
    for s in range(r_steps):
        a_s = a_ref[s]
        h = a_s * h + b_ref[s]
        p = a_s * p
        b_ref[s] = h
        a_ref[s] = p

    carry = ch_ref[0:1, :]
    for i in range(nr):
        cin_ref[i:i + 1, :] = carry
        carry = a_ref[r_steps - 1, i:i + 1, :] * carry + b_ref[r_steps - 1, i:i + 1, :]
    ch_ref[0:1, :] = carry

    cin = cin_ref[...]
    for s in range(r_steps):
        h_s = b_ref[s] + a_ref[s] * cin
        ya_ref[:, s * c:(s + 1) * c] = h_s * _silu(ga(s))

    for j in (1, 2, 3):
        cx_ref[j - 1:j, :] = xa(r_steps - j)[nr - 1:nr, :]

    @pl.when(ti == last)
    def _():
        hl_ref[...] = carry
        for j in range(CONV_W - 1):
            conv_ref[j:j + 1, :] = xa(r_steps - (CONV_W - 1) + j)[nr - 1:nr, :]


def _lru_prompt(za, cw, cb, wg, bg, lam):
    r, nr, c = CHUNK, CHUNK_ROWS, D_LRU
    nb, nrows, _ = za.shape
    assert nrows % nr == 0
    ya, conv_new, h_last = pl.pallas_call(
        functools.partial(_lru_prompt_kernel, r, nr),
        grid=(nb, nrows // nr),
        in_specs=[_tile3(nr, r * 2 * c),
                  _full((CONV_W, c)), _full((1, c)), _full((c, 2 * c)), _full((1, 2 * c)),
                  _full((1, c))],
        out_specs=[_tile3(nr, r * c),
                   pl.BlockSpec((None, CONV_W - 1, c), lambda b, t: (b, 0, 0)),
                   pl.BlockSpec((None, 1, c), lambda b, t: (b, 0, 0))],
        out_shape=[jax.ShapeDtypeStruct((nb, nrows, r * c), F32),
                   jax.ShapeDtypeStruct((nb, CONV_W - 1, c), F32),
                   jax.ShapeDtypeStruct((nb, 1, c), F32)],
        scratch_shapes=[pltpu.VMEM((r, nr, c), F32), pltpu.VMEM((r, nr, c), F32),
                        pltpu.VMEM((r, nr, c), F32), pltpu.VMEM((nr, c), F32),
                        pltpu.VMEM((8, c), F32), pltpu.VMEM((8, c), F32)],
        compiler_params=_cparams(2),
        name="lru_prompt",
    )(za, cw, cb, wg, bg, lam)
    return ya, conv_new, h_last.reshape(nb, c)


def _lru_sample_kernel(r_steps, za_ref, sc_ref, h0_ref, cw_ref, cb_ref, wg_ref, bg_ref, lam_ref,
                       ya_ref, conv_ref, hl_ref, xc_ref, a_ref, b_ref):
    c = D_LRU
    nr = za_ref.shape[0]

    def xa(s):
        return za_ref[:, s * 2 * c:s * 2 * c + c]

    def ga(s):
        return za_ref[:, s * 2 * c + c:(s + 1) * 2 * c]

    prev = {j: sc_ref[:, (CONV_W - 1 - j) * c:(CONV_W - j) * c] for j in (1, 2, 3)}
    _lru_conv(xa, prev, cw_ref, cb_ref, xc_ref, r_steps)
    xc = xc_ref[...].reshape(r_steps * nr, c)
    a, mult, gate_i = _lru_gates(xc, wg_ref, bg_ref, lam_ref)
    a_ref[...] = a.reshape(r_steps, nr, c)
    b_ref[...] = (mult * gate_i * xc).reshape(r_steps, nr, c)
    h = h0_ref[...]
    for s in range(r_steps):
        h = a_ref[s] * h + b_ref[s]
        ya_ref[:, s * c:(s + 1) * c] = h * _silu(ga(s))
    hl_ref[...] = h
    for j in range(CONV_W - 1):
        conv_ref[:, j * c:(j + 1) * c] = xa(r_steps - (CONV_W - 1) + j)


def _lru_sample(za, nb, seq_len, state_conv, h0, cw, cb, wg, bg, lam):
    r, c = seq_len, D_LRU
    zav = za.reshape(nb, r * 2 * c)
    scv = state_conv.reshape(nb, (CONV_W - 1) * c)
    ya, conv_new, h_last = pl.pallas_call(
        functools.partial(_lru_sample_kernel, r),
        grid=(1,),
        in_specs=[_full((nb, r * 2 * c)), _full((nb, (CONV_W - 1) * c)), _full((nb, c)),
                  _full((CONV_W, c)), _full((1, c)), _full((c, 2 * c)), _full((1, 2 * c)),
                  _full((1, c))],
        out_specs=[_full((nb, r * c)), _full((nb, (CONV_W - 1) * c)), _full((nb, c))],
        out_shape=[jax.ShapeDtypeStruct((nb, r * c), F32),
                   jax.ShapeDtypeStruct((nb, (CONV_W - 1) * c), F32),
                   jax.ShapeDtypeStruct((nb, c), F32)],
        scratch_shapes=[pltpu.VMEM((r, nb, c), F32), pltpu.VMEM((r, nb, c), F32),
                        pltpu.VMEM((r, nb, c), F32)],
        compiler_params=_cparams(1),
        name="lru_sample",
    )(zav, scv, h0, cw, cb, wg, bg, lam)
    return ya.reshape(nb * seq_len, c), conv_new.reshape(nb, CONV_W - 1, c), h_last


def _ssm_disc_kernel(r_steps, lr_ref, li_ref, ldt_ref, br_ref, bi_ref,
                     ab_ref, bbr_ref, bbi_ref, apow_ref):
    lr, li = lr_ref[...], li_ref[...]
    dt = jnp.exp(ldt_ref[...])
    mag = jnp.exp(lr * dt)
    ang = li * dt
    ab_re, ab_im = mag * jnp.cos(ang), mag * jnp.sin(ang)
    den = lr * lr + li * li
    xr, yi = ab_re - 1.0, ab_im
    coef_re = (xr * lr + yi * li) / den
    coef_im = (yi * lr - xr * li) / den
    br, bi = br_ref[...], bi_ref[...]
    bbr_ref[...] = coef_re * br - coef_im * bi
    bbi_ref[...] = coef_re * bi + coef_im * br
    ab_ref[0:1, :] = ab_re
    ab_ref[1:2, :] = ab_im
    pr, pi = ab_re, ab_im
    for s in range(r_steps):
        apow_ref[s:s + 1, 0:N_STATE] = pr
        apow_ref[s:s + 1, N_STATE:2 * N_STATE] = pi
        pr, pi = pr * ab_re - pi * ab_im, pr * ab_im + pi * ab_re


def _ssm_discretise(lam_re, lam_im, log_dt, b_re, b_im):
    def vec():
        return pl.BlockSpec((None, 1, N_STATE), lambda l: (l, 0, 0))

    def mat():
        return pl.BlockSpec((None, SSM_GROUP, N_STATE), lambda l: (l, 0, 0))

    return pl.pallas_call(
        functools.partial(_ssm_disc_kernel, CHUNK),
        grid=(DEPTH,),
        in_specs=[vec(), vec(), vec(), mat(), mat()],
        out_specs=[pl.BlockSpec((None, 2, N_STATE), lambda l: (l, 0, 0)), mat(), mat(),
                   pl.BlockSpec((None, CHUNK, 2 * N_STATE), lambda l: (l, 0, 0))],
        out_shape=[jax.ShapeDtypeStruct((DEPTH, 2, N_STATE), F32),
                   jax.ShapeDtypeStruct((DEPTH, SSM_GROUP, N_STATE), F32),
                   jax.ShapeDtypeStruct((DEPTH, SSM_GROUP, N_STATE), F32),
                   jax.ShapeDtypeStruct((DEPTH, CHUNK, 2 * N_STATE), F32)],
        compiler_params=_cparams(1),
        name="ssm_discretise",
    )(lam_re, lam_im, log_dt, b_re, b_im)


def _ssm_readout(x_all, u_all, g_all, cmat_ref, d_ref, gw_ref, gb_ref):
    y = jnp.dot(x_all, cmat_ref[...], preferred_element_type=F32) + d_ref[...] * u_all
    y = jax.nn.gelu(y)
    y = y * _sigmoid(jnp.dot(y.astype(BF16), gw_ref[...], preferred_element_type=F32) + gb_ref[...])
    return y * _silu(g_all)


def _ssm_prompt_kernel(r_steps, nr, zb_ref, bmat_ref, cmat_ref, ab_ref, apow_ref, d_ref,
                       gw_ref, gb_ref, yb_ref, sre_ref, sim_ref,
                       u_ref, g_ref, xs_ref, xb_ref, cin_ref, cst_ref):
    n, c = N_STATE, D_SSM
    ti = pl.program_id(1)
    last = pl.num_programs(1) - 1

    @pl.when(ti == 0)
    def _():
        cst_ref[...] = jnp.zeros_like(cst_ref)

    for s in range(r_steps):
        u_ref[s] = zb_ref[:, s * 2 * c:s * 2 * c + c]
        g_ref[s] = zb_ref[:, s * 2 * c + c:(s + 1) * 2 * c]
    u_all = u_ref[...].reshape(r_steps * nr, c)
    xs_ref[...] = jnp.dot(u_all.astype(BF16), bmat_ref[...],
                          preferred_element_type=F32).reshape(r_steps, nr, 2 * n)

    lane_block = 2 * LANES
    for lo in range(0, n, lane_block):
        hi = lo + lane_block
        ar, ai = ab_ref[0:1, lo:hi], ab_ref[1:2, lo:hi]
        xr = jnp.zeros((nr, lane_block), F32)
        xi = jnp.zeros((nr, lane_block), F32)
        for s in range(r_steps):
            nxr = ar * xr - ai * xi + xs_ref[s, :, lo:hi]
            nxi = ar * xi + ai * xr + xs_ref[s, :, n + lo:n + hi]
            xr, xi = nxr, nxi
            xs_ref[s, :, lo:hi] = xr
            xs_ref[s, :, n + lo:n + hi] = xi

    pr_end, pi_end = apow_ref[r_steps - 1:r_steps, 0:n], apow_ref[r_steps - 1:r_steps, n:2 * n]
    cr, ci = cst_ref[0:1, 0:n], cst_ref[0:1, n:2 * n]
    for i in range(nr):
        cin_ref[i:i + 1, 0:n] = cr
        cin_ref[i:i + 1, n:2 * n] = ci
        er = xs_ref[r_steps - 1, i:i + 1, 0:n]
        ei = xs_ref[r_steps - 1, i:i + 1, n:2 * n]
        cr, ci = pr_end * cr - pi_end * ci + er, pr_end * ci + pi_end * cr + ei
    cst_ref[0:1, 0:n] = cr
    cst_ref[0:1, n:2 * n] = ci

    for s in range(r_steps):
        pr, pi = apow_ref[s:s + 1, 0:n], apow_ref[s:s + 1, n:2 * n]
        cinr, cini = cin_ref[:, 0:n], cin_ref[:, n:2 * n]
        xb_ref[s, :, 0:n] = (xs_ref[s, :, 0:n] + pr * cinr - pi * cini).astype(BF16)
        xb_ref[s, :, n:2 * n] = (xs_ref[s, :, n:2 * n] + pr * cini + pi * cinr).astype(BF16)

    yb = _ssm_readout(xb_ref[...].reshape(r_steps * nr, 2 * n), u_all,
                      g_ref[...].reshape(r_steps * nr, c), cmat_ref, d_ref, gw_ref, gb_ref)
    for s in range(r_steps):
        yb_ref[:, s * c:(s + 1) * c] = yb[s * nr:(s + 1) * nr]

    @pl.when(ti == last)
    def _():
        sre_ref[...] = cr
        sim_ref[...] = ci


def _ssm_prompt(zb, bmat, cmat, ab, apow, d, gw, gb):
    r, nr, c, n = CHUNK, CHUNK_ROWS, D_SSM, N_STATE
    nb, nrows, _ = zb.shape
    yb, s_re, s_im = pl.pallas_call(
        functools.partial(_ssm_prompt_kernel, r, nr),
        grid=(nb, nrows // nr),
        in_specs=[_tile3(nr, r * 2 * c),
                  _full((c, 2 * n)), _full((2 * n, c)), _full((2, n)), _full((r, 2 * n)),
                  _full((1, c)), _full((c, c)), _full((1, c))],
        out_specs=[_tile3(nr, r * c),
                   pl.BlockSpec((None, 1, n), lambda b, t: (b, 0, 0)),
                   pl.BlockSpec((None, 1, n), lambda b, t: (b, 0, 0))],
        out_shape=[jax.ShapeDtypeStruct((nb, nrows, r * c), F32),
                   jax.ShapeDtypeStruct((nb, 1, n), F32),
                   jax.ShapeDtypeStruct((nb, 1, n), F32)],
        scratch_shapes=[pltpu.VMEM((r, nr, c), F32), pltpu.VMEM((r, nr, c), F32),
                        pltpu.VMEM((r, nr, 2 * n), F32), pltpu.VMEM((r, nr, 2 * n), BF16),
                        pltpu.VMEM((nr, 2 * n), F32), pltpu.VMEM((8, 2 * n), F32)],
        compiler_params=_cparams(2),
        name="ssm_prompt",
    )(zb, bmat, cmat, ab, apow, d, gw, gb)
    return yb, s_re.reshape(nb, n), s_im.reshape(nb, n)


def _ssm_sample_kernel(r_steps, zb_ref, s0r_ref, s0i_ref, bmat_ref, cmat_ref, ab_ref, d_ref,
                       gw_ref, gb_ref, yb_ref, sre_ref, sim_ref, u_ref, g_ref, xs_ref, xb_ref):
    n, c = N_STATE, D_SSM
    nr = zb_ref.shape[0]
    for s in range(r_steps):
        u_ref[s] = zb_ref[:, s * 2 * c:s * 2 * c + c]
        g_ref[s] = zb_ref[:, s * 2 * c + c:(s + 1) * 2 * c]
    u_all = u_ref[...].reshape(r_steps * nr, c)
    xs_ref[...] = jnp.dot(u_all.astype(BF16), bmat_ref[...],
                          preferred_element_type=F32).reshape(r_steps, nr, 2 * n)
    for lo in range(0, n, LANES):
        hi = lo + LANES
        ar, ai = ab_ref[0:1, lo:hi], ab_ref[1:2, lo:hi]
        xr, xi = s0r_ref[:, lo:hi], s0i_ref[:, lo:hi]
        for s in range(r_steps):
            nxr = ar * xr - ai * xi + xs_ref[s, :, lo:hi]
            nxi = ar * xi + ai * xr + xs_ref[s, :, n + lo:n + hi]
            xr, xi = nxr, nxi
            xb_ref[s, :, lo:hi] = xr.astype(BF16)
            xb_ref[s, :, n + lo:n + hi] = xi.astype(BF16)
        sre_ref[:, lo:hi] = xr
        sim_ref[:, lo:hi] = xi
    yb = _ssm_readout(xb_ref[...].reshape(r_steps * nr, 2 * n), u_all,
                      g_ref[...].reshape(r_steps * nr, c), cmat_ref, d_ref, gw_ref, gb_ref)
    for s in range(r_steps):
        yb_ref[:, s * c:(s + 1) * c] = yb[s * nr:(s + 1) * nr]


def _ssm_sample(zb, nb, seq_len, s0_re, s0_im, bmat, cmat, ab, d, gw, gb):
    r, c, n = seq_len, D_SSM, N_STATE
    zbv = zb.reshape(nb, r * 2 * c)
    yb, s_re, s_im = pl.pallas_call(
        functools.partial(_ssm_sample_kernel, r),
        grid=(1,),
        in_specs=[_full((nb, r * 2 * c)), _full((nb, n)), _full((nb, n)),
                  _full((c, 2 * n)), _full((2 * n, c)), _full((2, n)),
                  _full((1, c)), _full((c, c)), _full((1, c))],
        out_specs=[_full((nb, r * c)), _full((nb, n)), _full((nb, n))],
        out_shape=[jax.ShapeDtypeStruct((nb, r * c), F32),
                   jax.ShapeDtypeStruct((nb, n), F32),
                   jax.ShapeDtypeStruct((nb, n), F32)],
        scratch_shapes=[pltpu.VMEM((r, nb, c), F32), pltpu.VMEM((r, nb, c), F32),
                        pltpu.VMEM((r, nb, 2 * n), F32), pltpu.VMEM((r, nb, 2 * n), BF16)],
        compiler_params=_cparams(1),
        name="ssm_sample",
    )(zbv, s0_re, s0_im, bmat, cmat, ab, d, gw, gb)
    return yb.reshape(nb * seq_len, c), s_re, s_im


def _attend(q, kk, vv, bias_ref, first):
    blk = q.shape[0]
    lane = lax.broadcasted_iota(jnp.int32, (1, LANES), 1)
    outs, lses = [], []
    for pair in range(N_HEADS // 2):
        sl = slice(pair * LANES, (pair + 1) * LANES)
        qp, kp, vp = q[:, sl], kk[:, sl], vv[:, sl]
        acc = jnp.zeros((blk, LANES), F32)
        lse = jnp.zeros((blk, LANES), F32)
        for hh in range(2):
            head_lanes = (lane >= HEAD_DIM) if hh else (lane < HEAD_DIM)
            qh = jnp.where(head_lanes, qp, jnp.zeros_like(qp))
            s = lax.dot_general(qh, kp, (((1,), (1,)), ((), ())), preferred_element_type=F32)
            s = s + bias_ref[2 * pair + hh + N_HEADS * first]
            m = jnp.max(s, axis=-1, keepdims=True)
            e = jnp.exp2(s - m)
            den = jnp.sum(e, axis=-1, keepdims=True)
            vh = jnp.where(head_lanes, vp, jnp.zeros_like(vp))
            pv = jnp.dot(e.astype(BF16), vh, preferred_element_type=F32)
            acc = acc + pv * (1.0 / den)
            lse = jnp.where(head_lanes, m + jnp.log2(den), lse)
        outs.append(acc)
        lses.append(lse)
    return outs, lses


SUB_ROWS = ATT_BLOCK // (CHUNK // 4)
D1_ROWS = 2 * ATT_BLOCK // CHUNK
HALF_ROWS = D1_ROWS // 2


def _put_pairs(dst, lead, rows, vals, src_rows=None):
    for pair in range(N_HEADS // 2):
        v = vals[pair] if src_rows is None else vals[pair][src_rows]
        dst[lead, rows, pair * LANES:(pair + 1) * LANES] = v


def _attn_prompt_kernel(q_ref, k_ref, v_ref, b16_ref, b4_ref, b1_ref, o_ref,
                        qst, kst, vst, o16, l16, o4, l4, o1, l1, qq, kk, vv, qf, kf, vf, qq1, kk1, vv1):
    blk, c = ATT_BLOCK, D_ATT
    jb = pl.program_id(1)
    first = jnp.where(jb == 0, 1, 0)

    @pl.when(jb == 0)
    def _():
        for s in range(CHUNK):
            kst[s, 0:blk, :] = jnp.zeros((blk, c), BF16)
            vst[s, 0:blk, :] = jnp.zeros((blk, c), BF16)

    for s in range(CHUNK):
        sl = slice(s * c, (s + 1) * c)
        qst[s] = q_ref[:, sl]
        kst[s, blk:2 * blk, :] = k_ref[:, sl]
        vst[s, blk:2 * blk, :] = v_ref[:, sl]

    def d16_body(s, carry):
        outs, lses = _attend(qst.at[s], kst.at[s], vst.at[s], b16_ref, first)
        _put_pairs(o16, s, slice(None), outs)
        _put_pairs(l16, s, slice(None), lses)
        return carry

    lax.fori_loop(0, CHUNK, d16_body, 0)

    def sub_body(j, carry):
        r0 = pl.multiple_of(j * SUB_ROWS, SUB_ROWS)
        first_j = first * jnp.where(j == 0, 1, 0)

        for r in range(4):
            for m in range(CHUNK // 4):
                s = r + 4 * m
                qq[m * SUB_ROWS:(m + 1) * SUB_ROWS, :] = qst[s, pl.ds(r0, SUB_ROWS), :]
                kk[2 * m * SUB_ROWS:2 * (m + 1) * SUB_ROWS, :] = kst[s, pl.ds(blk - SUB_ROWS + r0, 2 * SUB_ROWS), :]
                vv[2 * m * SUB_ROWS:2 * (m + 1) * SUB_ROWS, :] = vst[s, pl.ds(blk - SUB_ROWS + r0, 2 * SUB_ROWS), :]
            outs, lses = _attend(qq, kk, vv, b4_ref, first_j)
            for m in range(CHUNK // 4):
                _put_pairs(o4, r + 4 * m, slice(None), outs, slice(m * SUB_ROWS, (m + 1) * SUB_ROWS))
                _put_pairs(l4, r + 4 * m, slice(None), lses, slice(m * SUB_ROWS, (m + 1) * SUB_ROWS))

        for t2 in range(SUB_ROWS // D1_ROWS):
            rt = r0 + t2 * D1_ROWS
            for s in range(CHUNK):
                qp = qst[s, pl.ds(pl.multiple_of(rt, D1_ROWS), D1_ROWS), :].astype(F32)
                for part in range(2):
                    qf[part * blk + s * HALF_ROWS:part * blk + (s + 1) * HALF_ROWS, :] = (
                        qp[part * HALF_ROWS:(part + 1) * HALF_ROWS])
                for src, dst in ((kst, kf), (vst, vf)):
                    kp = src[s, pl.ds(pl.multiple_of(blk - D1_ROWS + rt, D1_ROWS), 2 * D1_ROWS), :].astype(F32)
                    for part in range(3):
                        dst[part * blk + s * HALF_ROWS:part * blk + (s + 1) * HALF_ROWS, :] = (
                            kp[(part + 1) * HALF_ROWS:(part + 2) * HALF_ROWS])
            qq1[...] = qf[...].astype(BF16)
            kk1[...] = kf[...].astype(BF16)
            vv1[...] = vf[...].astype(BF16)
            for part in range(2):
                sel = first_j if (t2 == 0 and part == 0) else 0
                outs, lses = _attend(qq1.at[pl.ds(part * blk, blk), :], kk1.at[pl.ds(part * blk, 2 * blk), :],
                                     vv1.at[pl.ds(part * blk, 2 * blk), :], b1_ref, sel)
                rows = slice(t2 * D1_ROWS + part * HALF_ROWS, t2 * D1_ROWS + (part + 1) * HALF_ROWS)
                for s in range(CHUNK):
                    _put_pairs(o1, s, rows, outs, slice(s * HALF_ROWS, (s + 1) * HALF_ROWS))
                    _put_pairs(l1, s, rows, lses, slice(s * HALF_ROWS, (s + 1) * HALF_ROWS))

        for s in range(CHUNK):
            for lo in range(0, c, LANES):
                ls = [l16[s, pl.ds(r0, SUB_ROWS), lo:lo + LANES], l4[s, :, lo:lo + LANES], l1[s, :, lo:lo + LANES]]
                os = [o16[s, pl.ds(r0, SUB_ROWS), lo:lo + LANES], o4[s, :, lo:lo + LANES], o1[s, :, lo:lo + LANES]]
                top = jnp.maximum(jnp.maximum(ls[0], ls[1]), ls[2])
                ws = [jnp.exp2(l - top) for l in ls]
                mix = (ws[0] * os[0] + ws[1] * os[1] + ws[2] * os[2]) / (ws[0] + ws[1] + ws[2])
                o_ref[pl.ds(r0, SUB_ROWS), s * c + lo:s * c + lo + LANES] = mix
        return carry

    lax.fori_loop(0, blk // SUB_ROWS, sub_body, 0)

    for s in range(CHUNK):
        for lo in range(0, c, LANES):
            kst[s, 0:blk, lo:lo + LANES] = kst[s, blk:2 * blk, lo:lo + LANES]
            vst[s, 0:blk, lo:lo + LANES] = vst[s, blk:2 * blk, lo:lo + LANES]


def _prompt_bias(q_idx, k_idx, dil):
    dist = q_idx[:, None] - k_idx[None, :]
    valid = (dist >= 0) & (dist <= ATT_J)
    bias = -(ALIBI_SLOPES * LOG2_E)[:, None, None] * (dist * dil).astype(np.float32)[None]
    table = np.where(valid[None], bias, NEG_INF)
    masked = np.where((k_idx >= 0)[None, None, :], table, NEG_INF)
    return jnp.asarray(np.concatenate([table, masked], axis=0), dtype=F32)


def _attn_prompt(q, k, v):
    blk, c = ATT_BLOCK, D_ATT
    nb, nrows, width = q.shape
    assert nrows % blk == 0
    a = np.arange(blk)
    c2 = np.arange(2 * blk)
    b16 = _prompt_bias(a, c2 - blk, 16)
    k4 = 4 * (c2 % SUB_ROWS + SUB_ROWS * ((c2 // SUB_ROWS) % 2 - 1)) + c2 // (2 * SUB_ROWS)
    b4 = _prompt_bias(4 * (a % SUB_ROWS) + a // SUB_ROWS, k4, 4)
    order1 = CHUNK * (a % HALF_ROWS) + a // HALF_ROWS
    b1 = _prompt_bias(order1, np.concatenate([order1 - blk, order1]), 1)
    spec = _tile3(blk, width)
    bias_spec = _full((2 * N_HEADS, blk, 2 * blk))

    def st(rows, dtype):
        return pltpu.VMEM((CHUNK, rows, c), dtype)

    return pl.pallas_call(
        _attn_prompt_kernel, grid=(nb, nrows // blk),
        in_specs=[spec, spec, spec, bias_spec, bias_spec, bias_spec],
        out_specs=spec,
        out_shape=jax.ShapeDtypeStruct((nb, nrows, width), F32),
        scratch_shapes=[st(blk, BF16), st(2 * blk, BF16), st(2 * blk, BF16),
                        st(blk, F32), st(blk, F32),
                        st(SUB_ROWS, F32), st(SUB_ROWS, F32), st(SUB_ROWS, F32), st(SUB_ROWS, F32),
                        pltpu.VMEM((blk, c), BF16), pltpu.VMEM((2 * blk, c), BF16), pltpu.VMEM((2 * blk, c), BF16),
                        pltpu.VMEM((2 * blk, c), F32), pltpu.VMEM((3 * blk, c), F32), pltpu.VMEM((3 * blk, c), F32),
                        pltpu.VMEM((2 * blk, c), BF16), pltpu.VMEM((3 * blk, c), BF16), pltpu.VMEM((3 * blk, c), BF16)],
        compiler_params=_cparams(2), name="attn_prompt",
    )(q, k, v, b16, b4, b1)


def _attn_sample_kernel(bs, t_new, q_ref, kn_ref, vn_ref, kt_ref, vt_ref, bias_ref, o_ref):
    rows_qh = N_HEADS * t_new
    row_head = lax.broadcasted_iota(jnp.int32, (rows_qh, D_ATT), 0) // t_new
    lane_head = lax.broadcasted_iota(jnp.int32, (rows_qh, D_ATT), 1) // HEAD_DIM
    own = row_head == lane_head
    pad = jnp.zeros((LANES - t_new, D_ATT), F32)
    nt = (((1,), (1,)), ((), ()))
    for b in range(bs):
        kt = kt_ref[b].reshape(D_ATT, PAST_LEN).astype(BF16)
        vt = vt_ref[b].reshape(D_ATT, PAST_LEN).astype(BF16)
        kn = jnp.concatenate([kn_ref[b], pad], axis=0).astype(BF16)
        vn = jnp.concatenate([vn_ref[b], pad], axis=0).astype(BF16)
        q8 = q_ref[b].astype(F32)
        qbd = jnp.where(own, jnp.concatenate([q8] * N_HEADS, axis=0), 0.0).astype(BF16)
        s = jnp.concatenate([jnp.dot(qbd, kt, preferred_element_type=F32),
                             lax.dot_general(qbd, kn, nt, preferred_element_type=F32)], axis=1)
        es, lses = [], []
        for p in range(len(DILATED)):
            sp = s + bias_ref[p]
            m = jnp.max(sp, axis=-1, keepdims=True)
            e = jnp.exp2(sp - m)
            den = jnp.sum(e, axis=-1, keepdims=True)
            es.append(e / den)
            lses.append(m + jnp.log2(den))
        top = jnp.maximum(jnp.maximum(lses[0], lses[1]), lses[2])
        ws = [jnp.exp2(l - top) for l in lses]
        wsum = ws[0] + ws[1] + ws[2]
        pc = ((ws[0] * es[0] + ws[1] * es[1] + ws[2] * es[2]) / wsum).astype(BF16)
        o_all = (lax.dot_general(pc[:, :PAST_LEN], vt, nt, preferred_element_type=F32)
                 + jnp.dot(pc[:, PAST_LEN:], vn, preferred_element_type=F32))
        o_all = jnp.where(own, o_all, 0.0)
        out = o_all[0:t_new]
        for h in range(1, N_HEADS):
            out = out + o_all[h * t_new:(h + 1) * t_new]
        o_ref[b] = out


def _sample_bias(t_new):
    col = np.arange(PAST_LEN + LANES)
    real = col < PAST_LEN + t_new
    t = np.arange(t_new)
    dist = PAST_LEN + t[:, None] - col[None, :]
    out = []
    for window, dil in DILATED:
        valid = real[None, :] & (dist >= 0) & (dist % dil == 0) & (dist <= window)
        bias = -(ALIBI_SLOPES * LOG2_E)[:, None, None] * dist.astype(np.float32)[None]
        bias = np.where(valid[None], bias, NEG_INF)
        out.append(bias.reshape(N_HEADS * t_new, PAST_LEN + LANES))
    return jnp.asarray(np.stack(out, axis=0), dtype=F32)


def _attn_sample(q, k_new, v_new, cache_kt, cache_vt, layer, nb, t_new):
    bs = 2
    assert nb % bs == 0 and cache_kt.shape[-1] == PAST_LEN
    spec_c = pl.BlockSpec((None, bs, N_HEADS, HEAD_DIM, PAST_LEN), lambda i: (layer, i, 0, 0, 0))
    spec_new = pl.BlockSpec((bs, t_new, D_ATT), lambda i: (i, 0, 0))
    o = pl.pallas_call(
        functools.partial(_attn_sample_kernel, bs, t_new),
        grid=(nb // bs,),
        in_specs=[spec_new, spec_new, spec_new, spec_c, spec_c,
                  _full((len(DILATED), N_HEADS * t_new, PAST_LEN + LANES))],
        out_specs=spec_new,
        out_shape=jax.ShapeDtypeStruct((nb, t_new, D_ATT), F32),
        compiler_params=_cparams(1),
        name="attn_sample",
    )(q.reshape(nb, t_new, D_ATT), k_new.reshape(nb, t_new, D_ATT), v_new.reshape(nb, t_new, D_ATT),
      cache_kt, cache_vt, _sample_bias(t_new))
    return o.reshape(nb * t_new, D_ATT)


def _outproj_kernel(n_blk, x_ref, ya_ref, yb_ref, gc_ref, o_ref, og_ref, w_ref, out_ref, yn_ref):
    rows = x_ref.shape[0]
    a_hi, b_hi = D_LRU, D_LRU + D_SSM

    def norm_into(y, lo, hi, s):
        yn = y * lax.rsqrt(jnp.mean(y * y, axis=-1, keepdims=True) + EPS) * og_ref[:, lo:hi]
        yn_ref[s * rows:(s + 1) * rows, lo:hi] = yn.astype(BF16)

    for s in range(n_blk):
        att = slice(s * D_ATT, (s + 1) * D_ATT)
        yc = o_ref[:, att] * _silu(gc_ref[:, att])
        norm_into(ya_ref[:, s * D_LRU:(s + 1) * D_LRU], 0, a_hi, s)
        norm_into(yb_ref[:, s * D_SSM:(s + 1) * D_SSM], a_hi, b_hi, s)
        norm_into(yc, b_hi, D_MODEL, s)
    res = jnp.dot(yn_ref[...], w_ref[...], preferred_element_type=F32)
    for s in range(n_blk):
        sl = slice(s * D_MODEL, (s + 1) * D_MODEL)
        out_ref[:, sl] = x_ref[:, sl] + res[s * rows:(s + 1) * rows]


def _outproj(x3, n_blk, rows_t, ya, yb, gc, o, og, w_out_bf):
    nb, nrows, _ = x3.shape
    att_spec = _tile3(rows_t, n_blk * D_ATT)
    return pl.pallas_call(
        functools.partial(_outproj_kernel, n_blk),
        grid=(nb, nrows // rows_t),
        in_specs=[_tile3(rows_t, n_blk * D_MODEL), _tile3(rows_t, n_blk * D_LRU),
                  _tile3(rows_t, n_blk * D_SSM), att_spec, att_spec,
                  _full((1, D_MODEL)), _full((D_MODEL, D_MODEL))],
        out_specs=_tile3(rows_t, n_blk * D_MODEL),
        out_shape=jax.ShapeDtypeStruct(x3.shape, F32),
        scratch_shapes=[pltpu.VMEM((n_blk * rows_t, D_MODEL), BF16)],
        compiler_params=_cparams(2),
        name="outproj",
    )(x3, ya, yb, gc, o, og, w_out_bf)


def _block_diag(blocks):
    k, i, j = blocks.shape
    eye = jnp.eye(k, dtype=blocks.dtype)
    return jnp.einsum("kij,kl->kilj", blocks, eye).reshape(k * i, k * j)


def kernel(x_prompt, x_sample, state_conv, state_lru, state_ssm_re, state_ssm_im, cache_k, cache_v,
           norm_g, w_in, conv_w, conv_b, w_r, b_r, w_i, b_i, lru_lambda,
           ssm_lambda_re, ssm_lambda_im, ssm_log_dt, ssm_b_re, ssm_b_im, ssm_c_re, ssm_c_im,
           ssm_d, glu_w, glu_b, q_norm_g, k_norm_g, out_norm_g, w_out):
    bp, seq, _ = x_prompt.shape
    bs_, t_new, _ = x_sample.shape
    keep = min(PAST_LEN, seq)
    assert seq % (CHUNK * CHUNK_ROWS) == 0 and keep % (CHUNK * CHUNK_ROWS) == 0
    n_s = bs_ * t_new
    rows_s = min(ROW_TILE, n_s)
    xp = x_prompt.reshape(bp, seq // CHUNK, CHUNK * D_MODEL)
    xs = x_sample.reshape(1, n_s, D_MODEL)
    cache_kt = jnp.transpose(cache_k, (0, 1, 3, 4, 2))
    cache_vt = jnp.transpose(cache_v, (0, 1, 3, 4, 2))

    head_of = jnp.arange(D_ATT) // HEAD_DIM
    emat = (head_of[:, None] == head_of[None, :]).astype(BF16)

    def chan_major(b):
        return jnp.transpose(b, (0, 3, 1, 2)).reshape(DEPTH, SSM_GROUP, N_STATE)

    def vec(a):
        return a.reshape(DEPTH, 1, N_STATE)

    ab_all, bbr_all, bbi_all, apow_all = _ssm_discretise(
        vec(ssm_lambda_re), vec(ssm_lambda_im), vec(ssm_log_dt), chan_major(ssm_b_re), chan_major(ssm_b_im))

    sp_list, ss_list = [], []
    for l in range(DEPTH):
        w_in_bf = w_in[l].astype(BF16)
        w_out_bf = w_out[l].astype(BF16)
        ng = norm_g[l].reshape(1, D_MODEL)
        qg = jnp.tile(q_norm_g[l], N_HEADS).reshape(1, D_ATT)
        kg = jnp.tile(k_norm_g[l], N_HEADS).reshape(1, D_ATT)
        og = out_norm_g[l].reshape(1, D_MODEL)
        cw, cb = conv_w[l], conv_b[l].reshape(1, D_LRU)
        wg = jnp.concatenate([_block_diag(w_r[l]), _block_diag(w_i[l])], axis=1).astype(BF16)
        bg = jnp.concatenate([b_r[l], b_i[l]]).reshape(1, 2 * D_LRU)
        lam = lru_lambda[l].reshape(1, D_LRU)

        def b_matrix(bb):
            blocks = jnp.transpose(bb.reshape(SSM_GROUP, SSM_GROUPS, SSM_STATE), (1, 0, 2))
            return _block_diag(blocks)
        bmat = jnp.concatenate([b_matrix(bbr_all[l]), b_matrix(bbi_all[l])], axis=1).astype(BF16)
        cmat = jnp.concatenate([_block_diag(jnp.transpose(ssm_c_re[l], (0, 2, 1))),
                                -_block_diag(jnp.transpose(ssm_c_im[l], (0, 2, 1)))], axis=0).astype(BF16)
        d = ssm_d[l].reshape(1, D_SSM)
        gw = glu_w[l].astype(BF16)
        gb = glu_b[l].reshape(1, D_SSM)

        za, zb, q, k, v, gc, k_rows, v_rows = _inproj(xp, CHUNK, CHUNK_ROWS, keep // CHUNK,
                                                      ng, w_in_bf, qg, kg, emat)
        ya, conv_new, h_last = _lru_prompt(za, cw, cb, wg, bg, lam)
        yb, s_re, s_im = _ssm_prompt(zb, bmat, cmat, ab_all[l], apow_all[l], d, gw, gb)
        o = _attn_prompt(q, k, v)
        xp = _outproj(xp, CHUNK, CHUNK_ROWS, ya, yb, gc, o, og, w_out_bf)
        sp_list.append((conv_new, h_last,
                        s_re.reshape(bp, SSM_GROUPS, SSM_STATE), s_im.reshape(bp, SSM_GROUPS, SSM_STATE),
                        k_rows.reshape(bp, keep, N_HEADS, HEAD_DIM), v_rows.reshape(bp, keep, N_HEADS, HEAD_DIM)))

        za, zb, q, k, v, gc, k_rows, v_rows = _inproj(xs, 1, rows_s, n_s, ng, w_in_bf, qg, kg, emat)
        ya, conv_new, h_last = _lru_sample(za, bs_, t_new, state_conv[l], state_lru[l], cw, cb, wg, bg, lam)
        yb, s_re, s_im = _ssm_sample(zb, bs_, t_new, state_ssm_re[l].reshape(bs_, N_STATE),
                                     state_ssm_im[l].reshape(bs_, N_STATE), bmat, cmat, ab_all[l], d, gw, gb)
        o = _attn_sample(q, k_rows, v_rows, cache_kt, cache_vt, l, bs_, t_new)
        xs = _outproj(xs, 1, rows_s, ya.reshape(1, n_s, D_LRU), yb.reshape(1, n_s, D_SSM), gc,
                      o.reshape(1, n_s, D_ATT), og, w_out_bf)
        ss_list.append((conv_new, h_last,
                        s_re.reshape(bs_, SSM_GROUPS, SSM_STATE), s_im.reshape(bs_, SSM_GROUPS, SSM_STATE),
                        k_rows.reshape(bs_, t_new, N_HEADS, HEAD_DIM), v_rows.reshape(bs_, t_new, N_HEADS, HEAD_DIM)))

    def stack(lst, i):
        return jnp.stack([s[i] for s in lst], axis=0)

    return (xp.reshape(bp, seq, D_MODEL), xs.reshape(bs_, t_new, D_MODEL),
            stack(sp_list, 0), stack(sp_list, 1), stack(sp_list, 2), stack(sp_list, 3),
            stack(sp_list, 4), stack(sp_list, 5),
            stack(ss_list, 0), stack(ss_list, 1), stack(ss_list, 2), stack(ss_list, 3),
            stack(ss_list, 4), stack(ss_list, 5))
```

```python
import functools

import numpy as np
import jax
import jax.numpy as jnp
from jax import lax
from jax.experimental import pallas as pl
from jax.experimental.pallas import tpu as pltpu

F32 = jnp.float32
BF16 = jnp.bfloat16

D_MODEL = 1024
DEPTH = 4
PAST_LEN = 2048
D_LRU = 384
CONV_W = 4
RG_C = 8.0
D_SSM = 256
SSM_GROUP = 16
SSM_GROUPS = 16
SSM_STATE = 64
N_STATE = SSM_GROUPS * SSM_STATE
D_ATT = 384
HEAD_DIM = 64
N_HEADS = 6
DILATED = ((128, 1), (512, 4), (2048, 16))
ATT_J = 128
D_IN = 2 * D_LRU + 2 * D_SSM + 4 * D_ATT
EPS = 1e-6
NEG_INF = -1e30
LOG2_E = float(np.log2(np.e))
ALIBI_SLOPES = np.exp2(-8.0 * np.arange(1, N_HEADS + 1, dtype=np.float32) / N_HEADS).astype(np.float32)

C_ZA = (0, 2 * D_LRU)
C_ZB = (2 * D_LRU, 2 * D_LRU + 2 * D_SSM)
C_Q = (C_ZB[1], C_ZB[1] + D_ATT)
C_K = (C_Q[1], C_Q[1] + D_ATT)
C_V = (C_K[1], C_K[1] + D_ATT)
C_GC = (C_V[1], C_V[1] + D_ATT)

V7X_VMEM_LIMIT = 56 * 1024 * 1024
LANES = 128
CHUNK = 16
CHUNK_ROWS = 32
PROJ_ROWS = 64
ROW_TILE = 512
ATT_BLOCK = ATT_J


def _cparams(n_axes):
    return pltpu.CompilerParams(dimension_semantics=("arbitrary",) * n_axes,
                                vmem_limit_bytes=V7X_VMEM_LIMIT)


def _sigmoid(x):
    return 1.0 / (1.0 + jnp.exp(-x))


def _silu(x):
    return x * _sigmoid(x)


def _softplus(x):
    return jnp.maximum(x, 0.0) + jnp.log1p(jnp.exp(-jnp.abs(x)))


def _full(shape):
    nd = len(shape)
    return pl.BlockSpec(shape, lambda *_: (0,) * nd)


def _tile3(rows, width):
    return pl.BlockSpec((None, rows, width), lambda b, t: (b, t, 0))


def _inproj_kernel(n_blk, x_ref, g_ref, w_ref, qg_ref, kg_ref, e_ref,
                   za_ref, zb_ref, q_ref, k_ref, v_ref, gc_ref, kf_ref, vf_ref, hn_ref):
    rows = x_ref.shape[0]
    for s in range(n_blk):
        x = x_ref[:, s * D_MODEL:(s + 1) * D_MODEL]
        ms = jnp.mean(x * x, axis=-1, keepdims=True)
        hn_ref[s * rows:(s + 1) * rows, :] = (x * lax.rsqrt(ms + EPS) * g_ref[...]).astype(BF16)
    hn = hn_ref[...]

    def proj(cols):
        return jnp.dot(hn, w_ref[:, cols[0]:cols[1]], preferred_element_type=F32)

    def head_norm(z, gain):
        ss = jnp.dot((z * z).astype(BF16), e_ref[...], preferred_element_type=F32)
        return z * lax.rsqrt(ss * (1.0 / HEAD_DIM) + EPS) * gain

    def put(ref, z):
        c = z.shape[1]
        for s in range(n_blk):
            ref[:, s * c:(s + 1) * c] = z[s * rows:(s + 1) * rows, :].astype(ref.dtype)

    put(za_ref, proj(C_ZA))
    put(zb_ref, proj(C_ZB))
    q = head_norm(proj(C_Q), qg_ref[...]) * (HEAD_DIM ** -0.5 * LOG2_E)
    k = head_norm(proj(C_K), kg_ref[...])
    v = proj(C_V)
    put(q_ref, q)
    put(k_ref, k)
    put(v_ref, v)
    put(kf_ref, k)
    put(vf_ref, v)
    put(gc_ref, proj(C_GC))


def _inproj(x3, n_blk, rows_t, keep_rows, norm_g, w_in_bf, qg, kg, emat):
    nb, nrows, _ = x3.shape
    assert nrows % rows_t == 0 and keep_rows % rows_t == 0
    skip = (nrows - keep_rows) // rows_t
    kv_spec = pl.BlockSpec((None, rows_t, n_blk * D_ATT), lambda b, t: (b, jnp.maximum(t - skip, 0), 0))

    def out(c, dtype=F32):
        return jax.ShapeDtypeStruct((nb, nrows, n_blk * c), dtype)

    kv_shape = jax.ShapeDtypeStruct((nb, keep_rows, n_blk * D_ATT), F32)
    return pl.pallas_call(
        functools.partial(_inproj_kernel, n_blk),
        grid=(nb, nrows // rows_t),
        in_specs=[_tile3(rows_t, n_blk * D_MODEL), _full((1, D_MODEL)), _full((D_MODEL, D_IN)),
                  _full((1, D_ATT)), _full((1, D_ATT)), _full((D_ATT, D_ATT))],
        out_specs=[_tile3(rows_t, n_blk * 2 * D_LRU), _tile3(rows_t, n_blk * 2 * D_SSM),
                   _tile3(rows_t, n_blk * D_ATT), _tile3(rows_t, n_blk * D_ATT),
                   _tile3(rows_t, n_blk * D_ATT), _tile3(rows_t, n_blk * D_ATT), kv_spec, kv_spec],
        out_shape=[out(2 * D_LRU), out(2 * D_SSM), out(D_ATT, BF16), out(D_ATT, BF16),
                   out(D_ATT, BF16), out(D_ATT), kv_shape, kv_shape],
        scratch_shapes=[pltpu.VMEM((n_blk * rows_t, D_MODEL), BF16)],
        compiler_params=_cparams(2),
        name="inproj",
    )(x3, norm_g, w_in_bf, qg, kg, emat)


def _lru_conv(xa, prev, cw_ref, cb_ref, xc_ref, r_steps):
    for s in range(r_steps):
        acc = cb_ref[...] + cw_ref[3:4, :] * xa(s)
        for back in (1, 2, 3):
            src = xa(s - back) if s - back >= 0 else prev[back - s]
            acc = acc + cw_ref[3 - back:4 - back, :] * src
        xc_ref[s] = acc


def _lru_gates(xc, wg_ref, bg_ref, lam_ref):
    g = jnp.dot(xc.astype(BF16), wg_ref[...], preferred_element_type=F32) + bg_ref[...]
    r = _sigmoid(g[:, :D_LRU])
    i = _sigmoid(g[:, D_LRU:])
    log_a = -RG_C * r * _softplus(-lam_ref[...])
    a = jnp.exp(log_a)
    mult = jnp.sqrt(-jnp.tanh(log_a) * (a * a + 1.0))
    return a, mult, i


def _lru_prompt_kernel(r_steps, nr, za_ref, cw_ref, cb_ref, wg_ref, bg_ref, lam_ref,
                       ya_ref, conv_ref, hl_ref,
                       xc_ref, a_ref, b_ref, cin_ref, cx_ref, ch_ref):
    c = D_LRU
    ti = pl.program_id(1)
    last = pl.num_programs(1) - 1

    @pl.when(ti == 0)
    def _():
        cx_ref[...] = jnp.zeros_like(cx_ref)
        ch_ref[...] = jnp.zeros_like(ch_ref)

    def xa(s):
        return za_ref[:, s * 2 * c:s * 2 * c + c]

    def ga(s):
        return za_ref[:, s * 2 * c + c:(s + 1) * 2 * c]

    row = lax.broadcasted_iota(jnp.int32, (nr, c), 0)
    prev = {j: jnp.where(row == 0, cx_ref[j - 1:j, :], pltpu.roll(xa(r_steps - j), 1, 0))
            for j in (1, 2, 3)}
    _lru_conv(xa, prev, cw_ref, cb_ref, xc_ref, r_steps)

    xc = xc_ref[...].reshape(r_steps * nr, c)
    a, mult, gate_i = _lru_gates(xc, wg_ref, bg_ref, lam_ref)
    flat_row = lax.broadcasted_iota(jnp.int32, (r_steps * nr, c), 0)
    mult = jnp.where(jnp.logical_and(flat_row == 0, ti == 0), 1.0, mult)
    a_ref[...] = a.reshape(r_steps, nr, c)
    b_ref[...] = (mult * gate_i * xc).reshape(r_steps, nr, c)

    h = jnp.zeros((nr, c), F32)
    p = jnp.ones((nr, c), F32)
    for s in range(r_steps):
        a_s = a_ref[s]
        h = a_s * h + b_ref[s]
        p = a_s * p
        b_ref[s] = h
        a_ref[s] = p

    carry = ch_ref[0:1, :]
    for i in range(nr):
        cin_ref[i:i + 1, :] = carry
        carry = a_ref[r_steps - 1, i:i + 1, :] * carry + b_ref[r_steps - 1, i:i + 1, :]
    ch_ref[0:1, :] = carry

    cin = cin_ref[...]
    for s in range(r_steps):
        h_s = b_ref[s] + a_ref[s] * cin
        ya_ref[:, s * c:(s + 1) * c] = h_s * _silu(ga(s))

    for j in (1, 2, 3):
        cx_ref[j - 1:j, :] = xa(r_steps - j)[nr - 1:nr, :]

    @pl.when(ti == last)
    def _():
        hl_ref[...] = carry
        for j in range(CONV_W - 1):
            conv_ref[j:j + 1, :] = xa(r_steps - (CONV_W - 1) + j)[nr - 1:nr, :]


def _lru_prompt(za, cw, cb, wg, bg, lam):
    r, nr, c = CHUNK, CHUNK_ROWS, D_LRU
    nb, nrows, _ = za.shape
    assert nrows % nr == 0
    ya, conv_new, h_last = pl.pallas_call(
        functools.partial(_lru_prompt_kernel, r, nr),
        grid=(nb, nrows // nr),
        in_specs=[_tile3(nr, r * 2 * c),
                  _full((CONV_W, c)), _full((1, c)), _full((c, 2 * c)), _full((1, 2 * c)),
                  _full((1, c))],
        out_specs=[_tile3(nr, r * c),
                   pl.BlockSpec((None, CONV_W - 1, c), lambda b, t: (b, 0, 0)),
                   pl.BlockSpec((None, 1, c), lambda b, t: (b, 0, 0))],
        out_shape=[jax.ShapeDtypeStruct((nb, nrows, r * c), F32),
                   jax.ShapeDtypeStruct((nb, CONV_W - 1, c), F32),
                   jax.ShapeDtypeStruct((nb, 1, c), F32)],
        scratch_shapes=[pltpu.VMEM((r, nr, c), F32), pltpu.VMEM((r, nr, c), F32),
                        pltpu.VMEM((r, nr, c), F32), pltpu.VMEM((nr, c), F32),
                        pltpu.VMEM((8, c), F32), pltpu.VMEM((8, c), F32)],
        compiler_params=_cparams(2),
        name="lru_prompt",
    )(za, cw, cb, wg, bg, lam)
    return ya, conv_new, h_last.reshape(nb, c)


def _lru_sample_kernel(r_steps, za_ref, sc_ref, h0_ref, cw_ref, cb_ref, wg_ref, bg_ref, lam_ref,
                       ya_ref, conv_ref, hl_ref, xc_ref, a_ref, b_ref):
    c = D_LRU
    nr = za_ref.shape[0]

    def xa(s):
        return za_ref[:, s * 2 * c:s * 2 * c + c]

    def ga(s):
        return za_ref[:, s * 2 * c + c:(s + 1) * 2 * c]

    prev = {j: sc_ref[:, (CONV_W - 1 - j) * c:(CONV_W - j) * c] for j in (1, 2, 3)}
    _lru_conv(xa, prev, cw_ref, cb_ref, xc_ref, r_steps)
    xc = xc_ref[...].reshape(r_steps * nr, c)
    a, mult, gate_i = _lru_gates(xc, wg_ref, bg_ref, lam_ref)
    a_ref[...] = a.reshape(r_steps, nr, c)
    b_ref[...] = (mult * gate_i * xc).reshape(r_steps, nr, c)
    h = h0_ref[...]
    for s in range(r_steps):
        h = a_ref[s] * h + b_ref[s]
        ya_ref[:, s * c:(s + 1) * c] = h * _silu(ga(s))
    hl_ref[...] = h
    for j in range(CONV_W - 1):
        conv_ref[:, j * c:(j + 1) * c] = xa(r_steps - (CONV_W - 1) + j)


def _lru_sample(za, nb, seq_len, state_conv, h0, cw, cb, wg, bg, lam):
    r, c = seq_len, D_LRU
    zav = za.reshape(nb, r * 2 * c)
    scv = state_conv.reshape(nb, (CONV_W - 1) * c)
    ya, conv_new, h_last = pl.pallas_call(
        functools.partial(_lru_sample_kernel, r),
        grid=(1,),
        in_specs=[_full((nb, r * 2 * c)), _full((nb, (CONV_W - 1) * c)), _full((nb, c)),
                  _full((CONV_W, c)), _full((1, c)), _full((c, 2 * c)), _full((1, 2 * c)),
                  _full((1, c))],
        out_specs=[_full((nb, r * c)), _full((nb, (CONV_W - 1) * c)), _full((nb, c))],
        out_shape=[jax.ShapeDtypeStruct((nb, r * c), F32),
                   jax.ShapeDtypeStruct((nb, (CONV_W - 1) * c), F32),
                   jax.ShapeDtypeStruct((nb, c), F32)],
        scratch_shapes=[pltpu.VMEM((r, nb, c), F32), pltpu.VMEM((r, nb, c), F32),
                        pltpu.VMEM((r, nb, c), F32)],
        compiler_params=_cparams(1),
        name="lru_sample",
    )(zav, scv, h0, cw, cb, wg, bg, lam)
    return ya.reshape(nb * seq_len, c), conv_new.reshape(nb, CONV_W - 1, c), h_last


def _ssm_disc_kernel(r_steps, lr_ref, li_ref, ldt_ref, br_ref, bi_ref,
                     ab_ref, bbr_ref, bbi_ref, apow_ref):
    lr, li = lr_ref[...], li_ref[...]
    dt = jnp.exp(ldt_ref[...])
    mag = jnp.exp(lr * dt)
    ang = li * dt
    ab_re, ab_im = mag * jnp.cos(ang), mag * jnp.sin(ang)
    den = lr * lr + li * li
    xr, yi = ab_re - 1.0, ab_im
    coef_re = (xr * lr + yi * li) / den
    coef_im = (yi * lr - xr * li) / den
    br, bi = br_ref[...], bi_ref[...]
    bbr_ref[...] = coef_re * br - coef_im * bi
    bbi_ref[...] = coef_re * bi + coef_im * br
    ab_ref[0:1, :] = ab_re
    ab_ref[1:2, :] = ab_im
    pr, pi = ab_re, ab_im
    for s in range(r_steps):
        apow_ref[s:s + 1, 0:N_STATE] = pr
        apow_ref[s:s + 1, N_STATE:2 * N_STATE] = pi
        pr, pi = pr * ab_re - pi * ab_im, pr * ab_im + pi * ab_re


def _ssm_discretise(lam_re, lam_im, log_dt, b_re, b_im):
    def vec():
        return pl.BlockSpec((None, 1, N_STATE), lambda l: (l, 0, 0))

    def mat():
        return pl.BlockSpec((None, SSM_GROUP, N_STATE), lambda l: (l, 0, 0))

    return pl.pallas_call(
        functools.partial(_ssm_disc_kernel, CHUNK),
        grid=(DEPTH,),
        in_specs=[vec(), vec(), vec(), mat(), mat()],
        out_specs=[pl.BlockSpec((None, 2, N_STATE), lambda l: (l, 0, 0)), mat(), mat(),
                   pl.BlockSpec((None, CHUNK, 2 * N_STATE), lambda l: (l, 0, 0))],
        out_shape=[jax.ShapeDtypeStruct((DEPTH, 2, N_STATE), F32),
                   jax.ShapeDtypeStruct((DEPTH, SSM_GROUP, N_STATE), F32),
                   jax.ShapeDtypeStruct((DEPTH, SSM_GROUP, N_STATE), F32),
                   jax.ShapeDtypeStruct((DEPTH, CHUNK, 2 * N_STATE), F32)],
        compiler_params=_cparams(1),
        name="ssm_discretise",
    )(lam_re, lam_im, log_dt, b_re, b_im)


def _ssm_readout(x_all, u_all, g_all, cmat_ref, d_ref, gw_ref, gb_ref):
    y = jnp.dot(x_all, cmat_ref[...], preferred_element_type=F32) + d_ref[...] * u_all
    y = jax.nn.gelu(y)
    y = y * _sigmoid(jnp.dot(y.astype(BF16), gw_ref[...], preferred_element_type=F32) + gb_ref[...])
    return y * _silu(g_all)


def _ssm_prompt_kernel(r_steps, nr, zb_ref, bmat_ref, cmat_ref, ab_ref, apow_ref, d_ref,
                       gw_ref, gb_ref, yb_ref, sre_ref, sim_ref,
                       u_ref, g_ref, xs_ref, xb_ref, cin_ref, cst_ref):
    n, c = N_STATE, D_SSM
    ti = pl.program_id(1)
    last = pl.num_programs(1) - 1

    @pl.when(ti == 0)
    def _():
        cst_ref[...] = jnp.zeros_like(cst_ref)

    for s in range(r_steps):
        u_ref[s] = zb_ref[:, s * 2 * c:s * 2 * c + c]
        g_ref[s] = zb_ref[:, s * 2 * c + c:(s + 1) * 2 * c]
    u_all = u_ref[...].reshape(r_steps * nr, c)
    xs_ref[...] = jnp.dot(u_all.astype(BF16), bmat_ref[...],
                          preferred_element_type=F32).reshape(r_steps, nr, 2 * n)

    lane_block = 2 * LANES
    for lo in range(0, n, lane_block):
        hi = lo + lane_block
        ar, ai = ab_ref[0:1, lo:hi], ab_ref[1:2, lo:hi]
        xr = jnp.zeros((nr, lane_block), F32)
        xi = jnp.zeros((nr, lane_block), F32)
        for s in range(r_steps):
            nxr = ar * xr - ai * xi + xs_ref[s, :, lo:hi]
            nxi = ar * xi + ai * xr + xs_ref[s, :, n + lo:n + hi]
            xr, xi = nxr, nxi
            xs_ref[s, :, lo:hi] = xr
            xs_ref[s, :, n + lo:n + hi] = xi

    pr_end, pi_end = apow_ref[r_steps - 1:r_steps, 0:n], apow_ref[r_steps - 1:r_steps, n:2 * n]
    cr, ci = cst_ref[0:1, 0:n], cst_ref[0:1, n:2 * n]
    for i in range(nr):
        cin_ref[i:i + 1, 0:n] = cr
        cin_ref[i:i + 1, n:2 * n] = ci
        er = xs_ref[r_steps - 1, i:i + 1, 0:n]
        ei = xs_ref[r_steps - 1, i:i + 1, n:2 * n]
        cr, ci = pr_end * cr - pi_end * ci + er, pr_end * ci + pi_end * cr + ei
    cst_ref[0:1, 0:n] = cr
    cst_ref[0:1, n:2 * n] = ci

    for s in range(r_steps):
        pr, pi = apow_ref[s:s + 1, 0:n], apow_ref[s:s + 1, n:2 * n]
        cinr, cini = cin_ref[:, 0:n], cin_ref[:, n:2 * n]
        xb_ref[s, :, 0:n] = (xs_ref[s, :, 0:n] + pr * cinr - pi * cini).astype(BF16)
        xb_ref[s, :, n:2 * n] = (xs_ref[s, :, n:2 * n] + pr * cini + pi * cinr).astype(BF16)

    yb = _ssm_readout(xb_ref[...].reshape(r_steps * nr, 2 * n), u_all,
                      g_ref[...].reshape(r_steps * nr, c), cmat_ref, d_ref, gw_ref, gb_ref)
    for s in range(r_steps):
        yb_ref[:, s * c:(s + 1) * c] = yb[s * nr:(s + 1) * nr]

    @pl.when(ti == last)
    def _():
        sre_ref[...] = cr
        sim_ref[...] = ci


def _ssm_prompt(zb, bmat, cmat, ab, apow, d, gw, gb):
    r, nr, c, n = CHUNK, CHUNK_ROWS, D_SSM, N_STATE
    nb, nrows, _ = zb.shape
    yb, s_re, s_im = pl.pallas_call(
        functools.partial(_ssm_prompt_kernel, r, nr),
        grid=(nb, nrows // nr),
        in_specs=[_tile3(nr, r * 2 * c),
                  _full((c, 2 * n)), _full((2 * n, c)), _full((2, n)), _full((r, 2 * n)),
                  _full((1, c)), _full((c, c)), _full((1, c))],
        out_specs=[_tile3(nr, r * c),
                   pl.BlockSpec((None, 1, n), lambda b, t: (b, 0, 0)),
                   pl.BlockSpec((None, 1, n), lambda b, t: (b, 0, 0))],
        out_shape=[jax.ShapeDtypeStruct((nb, nrows, r * c), F32),
                   jax.ShapeDtypeStruct((nb, 1, n), F32),
                   jax.ShapeDtypeStruct((nb, 1, n), F32)],
        scratch_shapes=[pltpu.VMEM((r, nr, c), F32), pltpu.VMEM((r, nr, c), F32),
                        pltpu.VMEM((r, nr, 2 * n), F32), pltpu.VMEM((r, nr, 2 * n), BF16),
                        pltpu.VMEM((nr, 2 * n), F32), pltpu.VMEM((8, 2 * n), F32)],
        compiler_params=_cparams(2),
        name="ssm_prompt",
    )(zb, bmat, cmat, ab, apow, d, gw, gb)
    return yb, s_re.reshape(nb, n), s_im.reshape(nb, n)


def _ssm_sample_kernel(r_steps, zb_ref, s0r_ref, s0i_ref, bmat_ref, cmat_ref, ab_ref, d_ref,
                       gw_ref, gb_ref, yb_ref, sre_ref, sim_ref, u_ref, g_ref, xs_ref, xb_ref):
    n, c = N_STATE, D_SSM
    nr = zb_ref.shape[0]
    for s in range(r_steps):
        u_ref[s] = zb_ref[:, s * 2 * c:s * 2 * c + c]
        g_ref[s] = zb_ref[:, s * 2 * c + c:(s + 1) * 2 * c]
    u_all = u_ref[...].reshape(r_steps * nr, c)
    xs_ref[...] = jnp.dot(u_all.astype(BF16), bmat_ref[...],
                          preferred_element_type=F32).reshape(r_steps, nr, 2 * n)
    for lo in range(0, n, LANES):
        hi = lo + LANES
        ar, ai = ab_ref[0:1, lo:hi], ab_ref[1:2, lo:hi]
        xr, xi = s0r_ref[:, lo:hi], s0i_ref[:, lo:hi]
        for s in range(r_steps):
            nxr = ar * xr - ai * xi + xs_ref[s, :, lo:hi]
            nxi = ar * xi + ai * xr + xs_ref[s, :, n + lo:n + hi]
            xr, xi = nxr, nxi
            xb_ref[s, :, lo:hi] = xr.astype(BF16)
            xb_ref[s, :, n + lo:n + hi] = xi.astype(BF16)
        sre_ref[:, lo:hi] = xr
        sim_ref[:, lo:hi] = xi
    yb = _ssm_readout(xb_ref[...].reshape(r_steps * nr, 2 * n), u_all,
                      g_ref[...].reshape(r_steps * nr, c), cmat_ref, d_ref, gw_ref, gb_ref)
    for s in range(r_steps):
        yb_ref[:, s * c:(s + 1) * c] = yb[s * nr:(s + 1) * nr]


def _ssm_sample(zb, nb, seq_len, s0_re, s0_im, bmat, cmat, ab, d, gw, gb):
    r, c, n = seq_len, D_SSM, N_STATE
    zbv = zb.reshape(nb, r * 2 * c)
    yb, s_re, s_im = pl.pallas_call(
        functools.partial(_ssm_sample_kernel, r),
        grid=(1,),
        in_specs=[_full((nb, r * 2 * c)), _full((nb, n)), _full((nb, n)),
                  _full((c, 2 * n)), _full((2 * n, c)), _full((2, n)),
                  _full((1, c)), _full((c, c)), _full((1, c))],
        out_specs=[_full((nb, r * c)), _full((nb, n)), _full((nb, n))],
        out_shape=[jax.ShapeDtypeStruct((nb, r * c), F32),
                   jax.ShapeDtypeStruct((nb, n), F32),
                   jax.ShapeDtypeStruct((nb, n), F32)],
        scratch_shapes=[pltpu.VMEM((r, nb, c), F32), pltpu.VMEM((r, nb, c), F32),
                        pltpu.VMEM((r, nb, 2 * n), F32), pltpu.VMEM((r, nb, 2 * n), BF16)],
        compiler_params=_cparams(1),
        name="ssm_sample",
    )(zbv, s0_re, s0_im, bmat, cmat, ab, d, gw, gb)
    return yb.reshape(nb * seq_len, c), s_re, s_im


def _attend(q, kk, vv, bias_ref, first):
    blk = q.shape[0]
    lane = lax.broadcasted_iota(jnp.int32, (1, LANES), 1)
    outs, lses = [], []
    for pair in range(N_HEADS // 2):
        sl = slice(pair * LANES, (pair + 1) * LANES)
        qp, kp, vp = q[:, sl], kk[:, sl], vv[:, sl]
        acc = jnp.zeros((blk, LANES), F32)
        lse = jnp.zeros((blk, LANES), F32)
        for hh in range(2):
            head_lanes = (lane >= HEAD_DIM) if hh else (lane < HEAD_DIM)
            qh = jnp.where(head_lanes, qp, jnp.zeros_like(qp))
            s = lax.dot_general(qh, kp, (((1,), (1,)), ((), ())), preferred_element_type=F32)
            s = s + bias_ref[2 * pair + hh + N_HEADS * first]
            m = jnp.max(s, axis=-1, keepdims=True)
            e = jnp.exp2(s - m)
            den = jnp.sum(e, axis=-1, keepdims=True)
            vh = jnp.where(head_lanes, vp, jnp.zeros_like(vp))
            pv = jnp.dot(e.astype(BF16), vh, preferred_element_type=F32)
            acc = acc + pv * (1.0 / den)
            lse = jnp.where(head_lanes, m + jnp.log2(den), lse)
        outs.append(acc)
        lses.append(lse)
    return outs, lses


SUB_ROWS = ATT_BLOCK // (CHUNK // 4)
D1_ROWS = 2 * ATT_BLOCK // CHUNK
HALF_ROWS = D1_ROWS // 2
D16_UNROLL = 8


def _put_pairs(dst, lead, rows, vals, src_rows=None):
    for pair in range(N_HEADS // 2):
        v = vals[pair] if src_rows is None else vals[pair][src_rows]
        dst[lead, rows, pair * LANES:(pair + 1) * LANES] = v


def _attn_prompt_kernel(q_ref, k_ref, v_ref, b16_ref, b4_ref, b1_ref, o_ref,
                        qst, kst, vst, o16, l16, o4, l4, o1, l1, qq, kk, vv, qf, kf, vf, qq1, kk1, vv1):
    blk, c = ATT_BLOCK, D_ATT
    jb = pl.program_id(1)
    first = jnp.where(jb == 0, 1, 0)

    @pl.when(jb == 0)
    def _():
        for s in range(CHUNK):
            kst[s, 0:blk, :] = jnp.zeros((blk, c), BF16)
            vst[s, 0:blk, :] = jnp.zeros((blk, c), BF16)

    for s in range(CHUNK):
        sl = slice(s * c, (s + 1) * c)
        qst[s] = q_ref[:, sl]
        kst[s, blk:2 * blk, :] = k_ref[:, sl]
        vst[s, blk:2 * blk, :] = v_ref[:, sl]

    def d16_body(i, carry):
        for u in range(D16_UNROLL):
            s = i * D16_UNROLL + u
            outs, lses = _attend(qst.at[s], kst.at[s], vst.at[s], b16_ref, first)
            _put_pairs(o16, s, slice(None), outs)
            _put_pairs(l16, s, slice(None), lses)
        return carry

    lax.fori_loop(0, CHUNK // D16_UNROLL, d16_body, 0)

    def sub_body(j, carry):
        r0 = pl.multiple_of(j * SUB_ROWS, SUB_ROWS)
        first_j = first * jnp.where(j == 0, 1, 0)

        for r in range(4):
            for m in range(CHUNK // 4):
                s = r + 4 * m
                qq[m * SUB_ROWS:(m + 1) * SUB_ROWS, :] = qst[s, pl.ds(r0, SUB_ROWS), :]
                kk[2 * m * SUB_ROWS:2 * (m + 1) * SUB_ROWS, :] = kst[s, pl.ds(blk - SUB_ROWS + r0, 2 * SUB_ROWS), :]
                vv[2 * m * SUB_ROWS:2 * (m + 1) * SUB_ROWS, :] = vst[s, pl.ds(blk - SUB_ROWS + r0, 2 * SUB_ROWS), :]
            outs, lses = _attend(qq, kk, vv, b4_ref, first_j)
            for m in range(CHUNK // 4):
                _put_pairs(o4, r + 4 * m, slice(None), outs, slice(m * SUB_ROWS, (m + 1) * SUB_ROWS))
                _put_pairs(l4, r + 4 * m, slice(None), lses, slice(m * SUB_ROWS, (m + 1) * SUB_ROWS))

        for t2 in range(SUB_ROWS // D1_ROWS):
            rt = r0 + t2 * D1_ROWS
            for s in range(CHUNK):
                qp = qst[s, pl.ds(pl.multiple_of(rt, D1_ROWS), D1_ROWS), :].astype(F32)
                for part in range(2):
                    qf[part * blk + s * HALF_ROWS:part * blk + (s + 1) * HALF_ROWS, :] = (
                        qp[part * HALF_ROWS:(part + 1) * HALF_ROWS])
                for src, dst in ((kst, kf), (vst, vf)):
                    kp = src[s, pl.ds(pl.multiple_of(blk - D1_ROWS + rt, D1_ROWS), 2 * D1_ROWS), :].astype(F32)
                    for part in range(3):
                        dst[part * blk + s * HALF_ROWS:part * blk + (s + 1) * HALF_ROWS, :] = (
                            kp[(part + 1) * HALF_ROWS:(part + 2) * HALF_ROWS])
            qq1[...] = qf[...].astype(BF16)
            kk1[...] = kf[...].astype(BF16)
            vv1[...] = vf[...].astype(BF16)
            for part in range(2):
                sel = first_j if (t2 == 0 and part == 0) else 0
                outs, lses = _attend(qq1.at[pl.ds(part * blk, blk), :], kk1.at[pl.ds(part * blk, 2 * blk), :],
                                     vv1.at[pl.ds(part * blk, 2 * blk), :], b1_ref, sel)
                rows = slice(t2 * D1_ROWS + part * HALF_ROWS, t2 * D1_ROWS + (part + 1) * HALF_ROWS)
                for s in range(CHUNK):
                    _put_pairs(o1, s, rows, outs, slice(s * HALF_ROWS, (s + 1) * HALF_ROWS))
                    _put_pairs(l1, s, rows, lses, slice(s * HALF_ROWS, (s + 1) * HALF_ROWS))

        for s in range(CHUNK):
            for lo in range(0, c, LANES):
                ls = [l16[s, pl.ds(r0, SUB_ROWS), lo:lo + LANES], l4[s, :, lo:lo + LANES], l1[s, :, lo:lo + LANES]]
                os = [o16[s, pl.ds(r0, SUB_ROWS), lo:lo + LANES], o4[s, :, lo:lo + LANES], o1[s, :, lo:lo + LANES]]
                top = jnp.maximum(jnp.maximum(ls[0], ls[1]), ls[2])
                ws = [jnp.exp2(l - top) for l in ls]
                mix = (ws[0] * os[0] + ws[1] * os[1] + ws[2] * os[2]) / (ws[0] + ws[1] + ws[2])
                o_ref[pl.ds(r0, SUB_ROWS), s * c + lo:s * c + lo + LANES] = mix
        return carry

    lax.fori_loop(0, blk // SUB_ROWS, sub_body, 0)

    for s in range(CHUNK):
        for lo in range(0, c, LANES):
            kst[s, 0:blk, lo:lo + LANES] = kst[s, blk:2 * blk, lo:lo + LANES]
            vst[s, 0:blk, lo:lo + LANES] = vst[s, blk:2 * blk, lo:lo + LANES]


def _prompt_bias(q_idx, k_idx, dil):
    dist = q_idx[:, None] - k_idx[None, :]
    valid = (dist >= 0) & (dist <= ATT_J)
    bias = -(ALIBI_SLOPES * LOG2_E)[:, None, None] * (dist * dil).astype(np.float32)[None]
    table = np.where(valid[None], bias, NEG_INF)
    masked = np.where((k_idx >= 0)[None, None, :], table, NEG_INF)
    return jnp.asarray(np.concatenate([table, masked], axis=0), dtype=F32)


def _attn_prompt(q, k, v):
    blk, c = ATT_BLOCK, D_ATT
    nb, nrows, width = q.shape
    assert nrows % blk == 0
    a = np.arange(blk)
    c2 = np.arange(2 * blk)
    b16 = _prompt_bias(a, c2 - blk, 16)
    k4 = 4 * (c2 % SUB_ROWS + SUB_ROWS * ((c2 // SUB_ROWS) % 2 - 1)) + c2 // (2 * SUB_ROWS)
    b4 = _prompt_bias(4 * (a % SUB_ROWS) + a // SUB_ROWS, k4, 4)
    order1 = CHUNK * (a % HALF_ROWS) + a // HALF_ROWS
    b1 = _prompt_bias(order1, np.concatenate([order1 - blk, order1]), 1)
    spec = _tile3(blk, width)
    bias_spec = _full((2 * N_HEADS, blk, 2 * blk))

    def st(rows, dtype):
        return pltpu.VMEM((CHUNK, rows, c), dtype)

    return pl.pallas_call(
        _attn_prompt_kernel, grid=(nb, nrows // blk),
        in_specs=[spec, spec, spec, bias_spec, bias_spec, bias_spec],
        out_specs=spec,
        out_shape=jax.ShapeDtypeStruct((nb, nrows, width), F32),
        scratch_shapes=[st(blk, BF16), st(2 * blk, BF16), st(2 * blk, BF16),
                        st(blk, F32), st(blk, F32),
                        st(SUB_ROWS, F32), st(SUB_ROWS, F32), st(SUB_ROWS, F32), st(SUB_ROWS, F32),
                        pltpu.VMEM((blk, c), BF16), pltpu.VMEM((2 * blk, c), BF16), pltpu.VMEM((2 * blk, c), BF16),
                        pltpu.VMEM((2 * blk, c), F32), pltpu.VMEM((3 * blk, c), F32), pltpu.VMEM((3 * blk, c), F32),
                        pltpu.VMEM((2 * blk, c), BF16), pltpu.VMEM((3 * blk, c), BF16), pltpu.VMEM((3 * blk, c), BF16)],
        compiler_params=_cparams(2), name="attn_prompt",
    )(q, k, v, b16, b4, b1)


def _attn_sample_kernel(bs, t_new, q_ref, kn_ref, vn_ref, kt_ref, vt_ref, bias_ref, o_ref):
    rows_qh = N_HEADS * t_new
    row_head = lax.broadcasted_iota(jnp.int32, (rows_qh, D_ATT), 0) // t_new
    lane_head = lax.broadcasted_iota(jnp.int32, (rows_qh, D_ATT), 1) // HEAD_DIM
    own = row_head == lane_head
    pad = jnp.zeros((LANES - t_new, D_ATT), F32)
    nt = (((1,), (1,)), ((), ()))
    for b in range(bs):
        kt = kt_ref[b].reshape(D_ATT, PAST_LEN).astype(BF16)
        vt = vt_ref[b].reshape(D_ATT, PAST_LEN).astype(BF16)
        kn = jnp.concatenate([kn_ref[b], pad], axis=0).astype(BF16)
        vn = jnp.concatenate([vn_ref[b], pad], axis=0).astype(BF16)
        q8 = q_ref[b].astype(F32)
        qbd = jnp.where(own, jnp.concatenate([q8] * N_HEADS, axis=0), 0.0).astype(BF16)
        s = jnp.concatenate([jnp.dot(qbd, kt, preferred_element_type=F32),
                             lax.dot_general(qbd, kn, nt, preferred_element_type=F32)], axis=1)
        es, lses = [], []
        for p in range(len(DILATED)):
            sp = s + bias_ref[p]
            m = jnp.max(sp, axis=-1, keepdims=True)
            e = jnp.exp2(sp - m)
            den = jnp.sum(e, axis=-1, keepdims=True)
            es.append(e / den)
            lses.append(m + jnp.log2(den))
        top = jnp.maximum(jnp.maximum(lses[0], lses[1]), lses[2])
        ws = [jnp.exp2(l - top) for l in lses]
        wsum = ws[0] + ws[1] + ws[2]
        pc = ((ws[0] * es[0] + ws[1] * es[1] + ws[2] * es[2]) / wsum).astype(BF16)
        o_all = (lax.dot_general(pc[:, :PAST_LEN], vt, nt, preferred_element_type=F32)
                 + jnp.dot(pc[:, PAST_LEN:], vn, preferred_element_type=F32))
        o_all = jnp.where(own, o_all, 0.0)
        out = o_all[0:t_new]
        for h in range(1, N_HEADS):
            out = out + o_all[h * t_new:(h + 1) * t_new]
        o_ref[b] = out


def _sample_bias(t_new):
    col = np.arange(PAST_LEN + LANES)
    real = col < PAST_LEN + t_new
    t = np.arange(t_new)
    dist = PAST_LEN + t[:, None] - col[None, :]
    out = []
    for window, dil in DILATED:
        valid = real[None, :] & (dist >= 0) & (dist % dil == 0) & (dist <= window)
        bias = -(ALIBI_SLOPES * LOG2_E)[:, None, None] * dist.astype(np.float32)[None]
        bias = np.where(valid[None], bias, NEG_INF)
        out.append(bias.reshape(N_HEADS * t_new, PAST_LEN + LANES))
    return jnp.asarray(np.stack(out, axis=0), dtype=F32)


def _attn_sample(q, k_new, v_new, cache_kt, cache_vt, layer, nb, t_new):
    bs = 2
    assert nb % bs == 0 and cache_kt.shape[-1] == PAST_LEN
    spec_c = pl.BlockSpec((None, bs, N_HEADS, HEAD_DIM, PAST_LEN), lambda i: (layer, i, 0, 0, 0))
    spec_new = pl.BlockSpec((bs, t_new, D_ATT), lambda i: (i, 0, 0))
    o = pl.pallas_call(
        functools.partial(_attn_sample_kernel, bs, t_new),
        grid=(nb // bs,),
        in_specs=[spec_new, spec_new, spec_new, spec_c, spec_c,
                  _full((len(DILATED), N_HEADS * t_new, PAST_LEN + LANES))],
        out_specs=spec_new,
        out_shape=jax.ShapeDtypeStruct((nb, t_new, D_ATT), F32),
        compiler_params=_cparams(1),
        name="attn_sample",
    )(q.reshape(nb, t_new, D_ATT), k_new.reshape(nb, t_new, D_ATT), v_new.reshape(nb, t_new, D_ATT),
      cache_kt, cache_vt, _sample_bias(t_new))
    return o.reshape(nb * t_new, D_ATT)


def _outproj_kernel(n_blk, x_ref, ya_ref, yb_ref, gc_ref, o_ref, og_ref, w_ref, out_ref, yn_ref):
    rows = x_ref.shape[0]
    a_hi, b_hi = D_LRU, D_LRU + D_SSM

    def norm_into(y, lo, hi, s):
        yn = y * lax.rsqrt(jnp.mean(y * y, axis=-1, keepdims=True) + EPS) * og_ref[:, lo:hi]
        yn_ref[s * rows:(s + 1) * rows, lo:hi] = yn.astype(BF16)

    for s in range(n_blk):
        att = slice(s * D_ATT, (s + 1) * D_ATT)
        yc = o_ref[:, att] * _silu(gc_ref[:, att])
        norm_into(ya_ref[:, s * D_LRU:(s + 1) * D_LRU], 0, a_hi, s)
        norm_into(yb_ref[:, s * D_SSM:(s + 1) * D_SSM], a_hi, b_hi, s)
        norm_into(yc, b_hi, D_MODEL, s)
    res = jnp.dot(yn_ref[...], w_ref[...], preferred_element_type=F32)
    for s in range(n_blk):
        sl = slice(s * D_MODEL, (s + 1) * D_MODEL)
        out_ref[:, sl] = x_ref[:, sl] + res[s * rows:(s + 1) * rows]


def _outproj(x3, n_blk, rows_t, ya, yb, gc, o, og, w_out_bf):
    nb, nrows, _ = x3.shape
    att_spec = _tile3(rows_t, n_blk * D_ATT)
    return pl.pallas_call(
        functools.partial(_outproj_kernel, n_blk),
        grid=(nb, nrows // rows_t),
        in_specs=[_tile3(rows_t, n_blk * D_MODEL), _tile3(rows_t, n_blk * D_LRU),
                  _tile3(rows_t, n_blk * D_SSM), att_spec, att_spec,
                  _full((1, D_MODEL)), _full((D_MODEL, D_MODEL))],
        out_specs=_tile3(rows_t, n_blk * D_MODEL),
        out_shape=jax.ShapeDtypeStruct(x3.shape, F32),
        scratch_shapes=[pltpu.VMEM((n_blk * rows_t, D_MODEL), BF16)],
        compiler_params=_cparams(2),
        name="outproj",
    )(x3, ya, yb, gc, o, og, w_out_bf)


def _block_diag(blocks):
    k, i, j = blocks.shape
    eye = jnp.eye(k, dtype=blocks.dtype)
    return jnp.einsum("kij,kl->kilj", blocks, eye).reshape(k * i, k * j)


def kernel(x_prompt, x_sample, state_conv, state_lru, state_ssm_re, state_ssm_im, cache_k, cache_v,
           norm_g, w_in, conv_w, conv_b, w_r, b_r, w_i, b_i, lru_lambda,
           ssm_lambda_re, ssm_lambda_im, ssm_log_dt, ssm_b_re, ssm_b_im, ssm_c_re, ssm_c_im,
           ssm_d, glu_w, glu_b, q_norm_g, k_norm_g, out_norm_g, w_out):
    bp, seq, _ = x_prompt.shape
    bs_, t_new, _ = x_sample.shape
    keep = min(PAST_LEN, seq)
    assert seq % (CHUNK * PROJ_ROWS) == 0 and keep % (CHUNK * PROJ_ROWS) == 0
    n_s = bs_ * t_new
    rows_s = min(ROW_TILE, n_s)
    xp = x_prompt.reshape(bp, seq // CHUNK, CHUNK * D_MODEL)
    xs = x_sample.reshape(1, n_s, D_MODEL)
    cache_kt = jnp.transpose(cache_k, (0, 1, 3, 4, 2))
    cache_vt = jnp.transpose(cache_v, (0, 1, 3, 4, 2))

    head_of = jnp.arange(D_ATT) // HEAD_DIM
    emat = (head_of[:, None] == head_of[None, :]).astype(BF16)

    def chan_major(b):
        return jnp.transpose(b, (0, 3, 1, 2)).reshape(DEPTH, SSM_GROUP, N_STATE)

    def vec(a):
        return a.reshape(DEPTH, 1, N_STATE)

    ab_all, bbr_all, bbi_all, apow_all = _ssm_discretise(
        vec(ssm_lambda_re), vec(ssm_lambda_im), vec(ssm_log_dt), chan_major(ssm_b_re), chan_major(ssm_b_im))

    sp_list, ss_list = [], []
    for l in range(DEPTH):
        w_in_bf = w_in[l].astype(BF16)
        w_out_bf = w_out[l].astype(BF16)
        ng = norm_g[l].reshape(1, D_MODEL)
        qg = jnp.tile(q_norm_g[l], N_HEADS).reshape(1, D_ATT)
        kg = jnp.tile(k_norm_g[l], N_HEADS).reshape(1, D_ATT)
        og = out_norm_g[l].reshape(1, D_MODEL)
        cw, cb = conv_w[l], conv_b[l].reshape(1, D_LRU)
        wg = jnp.concatenate([_block_diag(w_r[l]), _block_diag(w_i[l])], axis=1).astype(BF16)
        bg = jnp.concatenate([b_r[l], b_i[l]]).reshape(1, 2 * D_LRU)
        lam = lru_lambda[l].reshape(1, D_LRU)

        def b_matrix(bb):
            blocks = jnp.transpose(bb.reshape(SSM_GROUP, SSM_GROUPS, SSM_STATE), (1, 0, 2))
            return _block_diag(blocks)
        bmat = jnp.concatenate([b_matrix(bbr_all[l]), b_matrix(bbi_all[l])], axis=1).astype(BF16)
        cmat = jnp.concatenate([_block_diag(jnp.transpose(ssm_c_re[l], (0, 2, 1))),
                                -_block_diag(jnp.transpose(ssm_c_im[l], (0, 2, 1)))], axis=0).astype(BF16)
        d = ssm_d[l].reshape(1, D_SSM)
        gw = glu_w[l].astype(BF16)
        gb = glu_b[l].reshape(1, D_SSM)

        za, zb, q, k, v, gc, k_rows, v_rows = _inproj(xp, CHUNK, PROJ_ROWS, keep // CHUNK,
                                                      ng, w_in_bf, qg, kg, emat)
        ya, conv_new, h_last = _lru_prompt(za, cw, cb, wg, bg, lam)
        yb, s_re, s_im = _ssm_prompt(zb, bmat, cmat, ab_all[l], apow_all[l], d, gw, gb)
        o = _attn_prompt(q, k, v)
        xp = _outproj(xp, CHUNK, PROJ_ROWS, ya, yb, gc, o, og, w_out_bf)
        sp_list.append((conv_new, h_last,
                        s_re.reshape(bp, SSM_GROUPS, SSM_STATE), s_im.reshape(bp, SSM_GROUPS, SSM_STATE),
                        k_rows.reshape(bp, keep, N_HEADS, HEAD_DIM), v_rows.reshape(bp, keep, N_HEADS, HEAD_DIM)))

        za, zb, q, k, v, gc, k_rows, v_rows = _inproj(xs, 1, rows_s, n_s, ng, w_in_bf, qg, kg, emat)
        ya, conv_new, h_last = _lru_sample(za, bs_, t_new, state_conv[l], state_lru[l], cw, cb, wg, bg, lam)
        yb, s_re, s_im = _ssm_sample(zb, bs_, t_new, state_ssm_re[l].reshape(bs_, N_STATE),
                                     state_ssm_im[l].reshape(bs_, N_STATE), bmat, cmat, ab_all[l], d, gw, gb)
        o = _attn_sample(q, k_rows, v_rows, cache_kt, cache_vt, l, bs_, t_new)
        xs = _outproj(xs, 1, rows_s, ya.reshape(1, n_s, D_LRU), yb.reshape(1, n_s, D_SSM), gc,
                      o.reshape(1, n_s, D_ATT), og, w_out_bf)
        ss_list.append((conv_new, h_last,
                        s_re.reshape(bs_, SSM_GROUPS, SSM_STATE), s_im.reshape(bs_, SSM_GROUPS, SSM_STATE),
                        k_rows.reshape(bs_, t_new, N_HEADS, HEAD_DIM), v_rows.reshape(bs_, t_new, N_HEADS, HEAD_DIM)))

    def stack(lst, i):
        return jnp.stack([s[i] for s in lst], axis=0)

    return (xp.reshape(bp, seq, D_MODEL), xs.reshape(bs_, t_new, D_MODEL),
            stack(sp_list, 0), stack(sp_list, 1), stack(sp_list, 2), stack(sp_list, 3),
            stack(sp_list, 4), stack(sp_list, 5),
            stack(ss_list, 0), stack(ss_list, 1), stack(ss_list, 2), stack(ss_list, 3),
            stack(ss_list, 4), stack(ss_list, 5))
```

```python
import functools

import numpy as np
import jax
import jax.numpy as jnp
from jax import lax
from jax.experimental import pallas as pl
from jax.experimental.pallas import tpu as pltpu

F32 = jnp.float32
BF16 = jnp.bfloat16

D_MODEL = 1024
DEPTH = 4
PAST_LEN = 2048
D_LRU = 384
CONV_W = 4
RG_C = 8.0
D_SSM = 256
SSM_GROUP = 16
SSM_GROUPS = 16
SSM_STATE = 64
N_STATE = SSM_GROUPS * SSM_STATE
D_ATT = 384
HEAD_DIM = 64
N_HEADS = 6
DILATED = ((128, 1), (512, 4), (2048, 16))
ATT_J = 128
D_IN = 2 * D_LRU + 2 * D_SSM + 4 * D_ATT
EPS = 1e-6
NEG_INF = -1e30
LOG2_E = float(np.log2(np.e))
ALIBI_SLOPES = np.exp2(-8.0 * np.arange(1, N_HEADS + 1, dtype=np.float32) / N_HEADS).astype(np.float32)

C_ZA = (0, 2 * D_LRU)
C_ZB = (2 * D_LRU, 2 * D_LRU + 2 * D_SSM)
C_Q = (C_ZB[1], C_ZB[1] + D_ATT)
C_K = (C_Q[1], C_Q[1] + D_ATT)
C_V = (C_K[1], C_K[1] + D_ATT)
C_GC = (C_V[1], C_V[1] + D_ATT)

V7X_VMEM_LIMIT = 56 * 1024 * 1024
LANES = 128
CHUNK = 16
CHUNK_ROWS = 32
PROJ_ROWS = 64
ROW_TILE = 512
ATT_BLOCK = ATT_J


def _cparams(n_axes):
    return pltpu.CompilerParams(dimension_semantics=("arbitrary",) * n_axes,
                                vmem_limit_bytes=V7X_VMEM_LIMIT)


def _sigmoid(x):
    return 1.0 / (1.0 + jnp.exp(-x))


def _silu(x):
    return x * _sigmoid(x)


def _softplus(x):
    return jnp.maximum(x, 0.0) + jnp.log1p(jnp.exp(-jnp.abs(x)))


def _full(shape):
    nd = len(shape)
    return pl.BlockSpec(shape, lambda *_: (0,) * nd)


def _tile3(rows, width):
    return pl.BlockSpec((None, rows, width), lambda b, t: (b, t, 0))


def _inproj_kernel(n_blk, x_ref, g_ref, w_ref, qg_ref, kg_ref, e_ref,
                   za_ref, zb_ref, q_ref, k_ref, v_ref, gc_ref, kf_ref, vf_ref, hn_ref):
    rows = x_ref.shape[0]
    for s in range(n_blk):
        x = x_ref[:, s * D_MODEL:(s + 1) * D_MODEL]
        ms = jnp.mean(x * x, axis=-1, keepdims=True)
        hn_ref[s * rows:(s + 1) * rows, :] = (x * lax.rsqrt(ms + EPS) * g_ref[...]).astype(BF16)
    hn = hn_ref[...]

    def proj(cols):
        return jnp.dot(hn, w_ref[:, cols[0]:cols[1]], preferred_element_type=F32)

    def head_norm(z, gain):
        ss = jnp.dot((z * z).astype(BF16), e_ref[...], preferred_element_type=F32)
        return z * lax.rsqrt(ss * (1.0 / HEAD_DIM) + EPS) * gain

    def put(ref, z):
        c = z.shape[1]
        for s in range(n_blk):
            ref[:, s * c:(s + 1) * c] = z[s * rows:(s + 1) * rows, :].astype(ref.dtype)

    put(za_ref, proj(C_ZA))
    put(zb_ref, proj(C_ZB))
    q = head_norm(proj(C_Q), qg_ref[...]) * (HEAD_DIM ** -0.5 * LOG2_E)
    k = head_norm(proj(C_K), kg_ref[...])
    v = proj(C_V)
    put(q_ref, q)
    put(k_ref, k)
    put(v_ref, v)
    put(kf_ref, k)
    put(vf_ref, v)
    put(gc_ref, proj(C_GC))


def _inproj(x3, n_blk, rows_t, keep_rows, norm_g, w_in_bf, qg, kg, emat):
    nb, nrows, _ = x3.shape
    assert nrows % rows_t == 0 and keep_rows % rows_t == 0
    skip = (nrows - keep_rows) // rows_t
    kv_spec = pl.BlockSpec((None, rows_t, n_blk * D_ATT), lambda b, t: (b, jnp.maximum(t - skip, 0), 0))

    def out(c, dtype=F32):
        return jax.ShapeDtypeStruct((nb, nrows, n_blk * c), dtype)

    kv_shape = jax.ShapeDtypeStruct((nb, keep_rows, n_blk * D_ATT), F32)
    return pl.pallas_call(
        functools.partial(_inproj_kernel, n_blk),
        grid=(nb, nrows // rows_t),
        in_specs=[_tile3(rows_t, n_blk * D_MODEL), _full((1, D_MODEL)), _full((D_MODEL, D_IN)),
                  _full((1, D_ATT)), _full((1, D_ATT)), _full((D_ATT, D_ATT))],
        out_specs=[_tile3(rows_t, n_blk * 2 * D_LRU), _tile3(rows_t, n_blk * 2 * D_SSM),
                   _tile3(rows_t, n_blk * D_ATT), _tile3(rows_t, n_blk * D_ATT),
                   _tile3(rows_t, n_blk * D_ATT), _tile3(rows_t, n_blk * D_ATT), kv_spec, kv_spec],
        out_shape=[out(2 * D_LRU), out(2 * D_SSM), out(D_ATT, BF16), out(D_ATT, BF16),
                   out(D_ATT, BF16), out(D_ATT), kv_shape, kv_shape],
        scratch_shapes=[pltpu.VMEM((n_blk * rows_t, D_MODEL), BF16)],
        compiler_params=_cparams(2),
        name="inproj",
    )(x3, norm_g, w_in_bf, qg, kg, emat)


def _lru_conv(xa, prev, cw_ref, cb_ref, xc_ref, r_steps):
    for s in range(r_steps):
        acc = cb_ref[...] + cw_ref[3:4, :] * xa(s)
        for back in (1, 2, 3):
            src = xa(s - back) if s - back >= 0 else prev[back - s]
            acc = acc + cw_ref[3 - back:4 - back, :] * src
        xc_ref[s] = acc


def _lru_gates(xc, wg_ref, bg_ref, lam_ref):
    g = jnp.dot(xc.astype(BF16), wg_ref[...], preferred_element_type=F32) + bg_ref[...]
    r = _sigmoid(g[:, :D_LRU])
    i = _sigmoid(g[:, D_LRU:])
    log_a = -RG_C * r * _softplus(-lam_ref[...])
    a = jnp.exp(log_a)
    mult = jnp.sqrt(-jnp.tanh(log_a) * (a * a + 1.0))
    return a, mult, i


def _lru_prompt_init(cx_ref, ch_ref):
    @pl.when(pl.program_id(1) == 0)
    def _():
        cx_ref[...] = jnp.zeros_like(cx_ref)
        ch_ref[...] = jnp.zeros_like(ch_ref)


def _lru_prompt_finish(r_steps, nr, za_ref, conv_ref, hl_ref, ch_ref):
    c = D_LRU

    @pl.when(pl.program_id(1) == pl.num_programs(1) - 1)
    def _():
        hl_ref[...] = ch_ref[0:1, :]
        for j in range(CONV_W - 1):
            s = r_steps - (CONV_W - 1) + j
            conv_ref[j:j + 1, :] = za_ref[nr - 1:nr, s * 2 * c:s * 2 * c + c]


def _lru_prompt_main(r_steps, nr, za_ref, cw_ref, cb_ref, wg_ref, bg_ref, lam_ref, ya_ref,
                     xc_ref, a_ref, b_ref, cin_ref, cx_ref, ch_ref):
    c = D_LRU
    ti = pl.program_id(1)

    def xa(s):
        return za_ref[:, s * 2 * c:s * 2 * c + c]

    def ga(s):
        return za_ref[:, s * 2 * c + c:(s + 1) * 2 * c]

    row = lax.broadcasted_iota(jnp.int32, (nr, c), 0)
    prev = {j: jnp.where(row == 0, cx_ref[j - 1:j, :], pltpu.roll(xa(r_steps - j), 1, 0))
            for j in (1, 2, 3)}
    _lru_conv(xa, prev, cw_ref, cb_ref, xc_ref, r_steps)

    xc = xc_ref[...].reshape(r_steps * nr, c)
    a, mult, gate_i = _lru_gates(xc, wg_ref, bg_ref, lam_ref)
    flat_row = lax.broadcasted_iota(jnp.int32, (r_steps * nr, c), 0)
    mult = jnp.where(jnp.logical_and(flat_row == 0, ti == 0), 1.0, mult)
    a_ref[...] = a.reshape(r_steps, nr, c)
    b_ref[...] = (mult * gate_i * xc).reshape(r_steps, nr, c)

    h = jnp.zeros((nr, c), F32)
    p = jnp.ones((nr, c), F32)
    for s in range(r_steps):
        a_s = a_ref[s]
        h = a_s * h + b_ref[s]
        p = a_s * p
        b_ref[s] = h
        a_ref[s] = p

    carry = ch_ref[0:1, :]
    for i in range(nr):
        cin_ref[i:i + 1, :] = carry
        carry = a_ref[r_steps - 1, i:i + 1, :] * carry + b_ref[r_steps - 1, i:i + 1, :]
    ch_ref[0:1, :] = carry

    cin = cin_ref[...]
    for s in range(r_steps):
        h_s = b_ref[s] + a_ref[s] * cin
        ya_ref[:, s * c:(s + 1) * c] = h_s * _silu(ga(s))

    for j in (1, 2, 3):
        cx_ref[j - 1:j, :] = xa(r_steps - j)[nr - 1:nr, :]


def _lru_sample_kernel(r_steps, za_ref, sc_ref, h0_ref, cw_ref, cb_ref, wg_ref, bg_ref, lam_ref,
                       ya_ref, conv_ref, hl_ref, xc_ref, a_ref, b_ref):
    c = D_LRU
    nr = za_ref.shape[0]

    def xa(s):
        return za_ref[:, s * 2 * c:s * 2 * c + c]

    def ga(s):
        return za_ref[:, s * 2 * c + c:(s + 1) * 2 * c]

    prev = {j: sc_ref[:, (CONV_W - 1 - j) * c:(CONV_W - j) * c] for j in (1, 2, 3)}
    _lru_conv(xa, prev, cw_ref, cb_ref, xc_ref, r_steps)
    xc = xc_ref[...].reshape(r_steps * nr, c)
    a, mult, gate_i = _lru_gates(xc, wg_ref, bg_ref, lam_ref)
    a_ref[...] = a.reshape(r_steps, nr, c)
    b_ref[...] = (mult * gate_i * xc).reshape(r_steps, nr, c)
    h = h0_ref[...]
    for s in range(r_steps):
        h = a_ref[s] * h + b_ref[s]
        ya_ref[:, s * c:(s + 1) * c] = h * _silu(ga(s))
    hl_ref[...] = h
    for j in range(CONV_W - 1):
        conv_ref[:, j * c:(j + 1) * c] = xa(r_steps - (CONV_W - 1) + j)


def _lru_sample(za, nb, seq_len, state_conv, h0, cw, cb, wg, bg, lam):
    r, c = seq_len, D_LRU
    zav = za.reshape(nb, r * 2 * c)
    scv = state_conv.reshape(nb, (CONV_W - 1) * c)
    ya, conv_new, h_last = pl.pallas_call(
        functools.partial(_lru_sample_kernel, r),
        grid=(1,),
        in_specs=[_full((nb, r * 2 * c)), _full((nb, (CONV_W - 1) * c)), _full((nb, c)),
                  _full((CONV_W, c)), _full((1, c)), _full((c, 2 * c)), _full((1, 2 * c)),
                  _full((1, c))],
        out_specs=[_full((nb, r * c)), _full((nb, (CONV_W - 1) * c)), _full((nb, c))],
        out_shape=[jax.ShapeDtypeStruct((nb, r * c), F32),
                   jax.ShapeDtypeStruct((nb, (CONV_W - 1) * c), F32),
                   jax.ShapeDtypeStruct((nb, c), F32)],
        scratch_shapes=[pltpu.VMEM((r, nb, c), F32), pltpu.VMEM((r, nb, c), F32),
                        pltpu.VMEM((r, nb, c), F32)],
        compiler_params=_cparams(1),
        name="lru_sample",
    )(zav, scv, h0, cw, cb, wg, bg, lam)
    return ya.reshape(nb * seq_len, c), conv_new.reshape(nb, CONV_W - 1, c), h_last


def _ssm_disc_kernel(r_steps, lr_ref, li_ref, ldt_ref, br_ref, bi_ref,
                     ab_ref, bbr_ref, bbi_ref, apow_ref):
    lr, li = lr_ref[...], li_ref[...]
    dt = jnp.exp(ldt_ref[...])
    mag = jnp.exp(lr * dt)
    ang = li * dt
    ab_re, ab_im = mag * jnp.cos(ang), mag * jnp.sin(ang)
    den = lr * lr + li * li
    xr, yi = ab_re - 1.0, ab_im
    coef_re = (xr * lr + yi * li) / den
    coef_im = (yi * lr - xr * li) / den
    br, bi = br_ref[...], bi_ref[...]
    bbr_ref[...] = coef_re * br - coef_im * bi
    bbi_ref[...] = coef_re * bi + coef_im * br
    ab_ref[0:1, :] = ab_re
    ab_ref[1:2, :] = ab_im
    pr, pi = ab_re, ab_im
    for s in range(r_steps):
        apow_ref[s:s + 1, 0:N_STATE] = pr
        apow_ref[s:s + 1, N_STATE:2 * N_STATE] = pi
        pr, pi = pr * ab_re - pi * ab_im, pr * ab_im + pi * ab_re


def _ssm_discretise(lam_re, lam_im, log_dt, b_re, b_im):
    def vec():
        return pl.BlockSpec((None, 1, N_STATE), lambda l: (l, 0, 0))

    def mat():
        return pl.BlockSpec((None, SSM_GROUP, N_STATE), lambda l: (l, 0, 0))

    return pl.pallas_call(
        functools.partial(_ssm_disc_kernel, CHUNK),
        grid=(DEPTH,),
        in_specs=[vec(), vec(), vec(), mat(), mat()],
        out_specs=[pl.BlockSpec((None, 2, N_STATE), lambda l: (l, 0, 0)), mat(), mat(),
                   pl.BlockSpec((None, CHUNK, 2 * N_STATE), lambda l: (l, 0, 0))],
        out_shape=[jax.ShapeDtypeStruct((DEPTH, 2, N_STATE), F32),
                   jax.ShapeDtypeStruct((DEPTH, SSM_GROUP, N_STATE), F32),
                   jax.ShapeDtypeStruct((DEPTH, SSM_GROUP, N_STATE), F32),
                   jax.ShapeDtypeStruct((DEPTH, CHUNK, 2 * N_STATE), F32)],
        compiler_params=_cparams(1),
        name="ssm_discretise",
    )(lam_re, lam_im, log_dt, b_re, b_im)


def _ssm_readout(x_all, u_all, g_all, cmat_ref, d_ref, gw_ref, gb_ref):
    y = jnp.dot(x_all, cmat_ref[...], preferred_element_type=F32) + d_ref[...] * u_all
    y = jax.nn.gelu(y)
    y = y * _sigmoid(jnp.dot(y.astype(BF16), gw_ref[...], preferred_element_type=F32) + gb_ref[...])
    return y * _silu(g_all)


def _ssm_prompt_init(cst_ref):
    @pl.when(pl.program_id(1) == 0)
    def _():
        cst_ref[...] = jnp.zeros_like(cst_ref)


def _ssm_prompt_finish(sre_ref, sim_ref, cst_ref):
    @pl.when(pl.program_id(1) == pl.num_programs(1) - 1)
    def _():
        sre_ref[...] = cst_ref[0:1, 0:N_STATE]
        sim_ref[...] = cst_ref[0:1, N_STATE:2 * N_STATE]


def _ssm_prompt_main(r_steps, nr, zb_ref, bmat_ref, cmat_ref, ab_ref, apow_ref, d_ref, gw_ref, gb_ref,
                     yb_ref, u_ref, g_ref, xs_ref, xb_ref, cin_ref, cst_ref):
    n, c = N_STATE, D_SSM

    for s in range(r_steps):
        u_ref[s] = zb_ref[:, s * 2 * c:s * 2 * c + c]
        g_ref[s] = zb_ref[:, s * 2 * c + c:(s + 1) * 2 * c]
    u_all = u_ref[...].reshape(r_steps * nr, c)
    xs_ref[...] = jnp.dot(u_all.astype(BF16), bmat_ref[...],
                          preferred_element_type=F32).reshape(r_steps, nr, 2 * n)

    lane_block = 2 * LANES
    for lo in range(0, n, lane_block):
        hi = lo + lane_block
        ar, ai = ab_ref[0:1, lo:hi], ab_ref[1:2, lo:hi]
        xr = jnp.zeros((nr, lane_block), F32)
        xi = jnp.zeros((nr, lane_block), F32)
        for s in range(r_steps):
            nxr = ar * xr - ai * xi + xs_ref[s, :, lo:hi]
            nxi = ar * xi + ai * xr + xs_ref[s, :, n + lo:n + hi]
            xr, xi = nxr, nxi
            xs_ref[s, :, lo:hi] = xr
            xs_ref[s, :, n + lo:n + hi] = xi

    pr_end, pi_end = apow_ref[r_steps - 1:r_steps, 0:n], apow_ref[r_steps - 1:r_steps, n:2 * n]
    cr, ci = cst_ref[0:1, 0:n], cst_ref[0:1, n:2 * n]
    for i in range(nr):
        cin_ref[i:i + 1, 0:n] = cr
        cin_ref[i:i + 1, n:2 * n] = ci
        er = xs_ref[r_steps - 1, i:i + 1, 0:n]
        ei = xs_ref[r_steps - 1, i:i + 1, n:2 * n]
        cr, ci = pr_end * cr - pi_end * ci + er, pr_end * ci + pi_end * cr + ei
    cst_ref[0:1, 0:n] = cr
    cst_ref[0:1, n:2 * n] = ci

    for s in range(r_steps):
        pr, pi = apow_ref[s:s + 1, 0:n], apow_ref[s:s + 1, n:2 * n]
        cinr, cini = cin_ref[:, 0:n], cin_ref[:, n:2 * n]
        xb_ref[s, :, 0:n] = (xs_ref[s, :, 0:n] + pr * cinr - pi * cini).astype(BF16)
        xb_ref[s, :, n:2 * n] = (xs_ref[s, :, n:2 * n] + pr * cini + pi * cinr).astype(BF16)

    yb = _ssm_readout(xb_ref[...].reshape(r_steps * nr, 2 * n), u_all,
                      g_ref[...].reshape(r_steps * nr, c), cmat_ref, d_ref, gw_ref, gb_ref)
    for s in range(r_steps):
        yb_ref[:, s * c:(s + 1) * c] = yb[s * nr:(s + 1) * nr]


def _ssm_sample_kernel(r_steps, zb_ref, s0r_ref, s0i_ref, bmat_ref, cmat_ref, ab_ref, d_ref,
                       gw_ref, gb_ref, yb_ref, sre_ref, sim_ref, u_ref, g_ref, xs_ref, xb_ref):
    n, c = N_STATE, D_SSM
    nr = zb_ref.shape[0]
    for s in range(r_steps):
        u_ref[s] = zb_ref[:, s * 2 * c:s * 2 * c + c]
        g_ref[s] = zb_ref[:, s * 2 * c + c:(s + 1) * 2 * c]
    u_all = u_ref[...].reshape(r_steps * nr, c)
    xs_ref[...] = jnp.dot(u_all.astype(BF16), bmat_ref[...],
                          preferred_element_type=F32).reshape(r_steps, nr, 2 * n)
    for lo in range(0, n, LANES):
        hi = lo + LANES
        ar, ai = ab_ref[0:1, lo:hi], ab_ref[1:2, lo:hi]
        xr, xi = s0r_ref[:, lo:hi], s0i_ref[:, lo:hi]
        for s in range(r_steps):
            nxr = ar * xr - ai * xi + xs_ref[s, :, lo:hi]
            nxi = ar * xi + ai * xr + xs_ref[s, :, n + lo:n + hi]
            xr, xi = nxr, nxi
            xb_ref[s, :, lo:hi] = xr.astype(BF16)
            xb_ref[s, :, n + lo:n + hi] = xi.astype(BF16)
        sre_ref[:, lo:hi] = xr
        sim_ref[:, lo:hi] = xi
    yb = _ssm_readout(xb_ref[...].reshape(r_steps * nr, 2 * n), u_all,
                      g_ref[...].reshape(r_steps * nr, c), cmat_ref, d_ref, gw_ref, gb_ref)
    for s in range(r_steps):
        yb_ref[:, s * c:(s + 1) * c] = yb[s * nr:(s + 1) * nr]


def _ssm_sample(zb, nb, seq_len, s0_re, s0_im, bmat, cmat, ab, d, gw, gb):
    r, c, n = seq_len, D_SSM, N_STATE
    zbv = zb.reshape(nb, r * 2 * c)
    yb, s_re, s_im = pl.pallas_call(
        functools.partial(_ssm_sample_kernel, r),
        grid=(1,),
        in_specs=[_full((nb, r * 2 * c)), _full((nb, n)), _full((nb, n)),
                  _full((c, 2 * n)), _full((2 * n, c)), _full((2, n)),
                  _full((1, c)), _full((c, c)), _full((1, c))],
        out_specs=[_full((nb, r * c)), _full((nb, n)), _full((nb, n))],
        out_shape=[jax.ShapeDtypeStruct((nb, r * c), F32),
                   jax.ShapeDtypeStruct((nb, n), F32),
                   jax.ShapeDtypeStruct((nb, n), F32)],
        scratch_shapes=[pltpu.VMEM((r, nb, c), F32), pltpu.VMEM((r, nb, c), F32),
                        pltpu.VMEM((r, nb, 2 * n), F32), pltpu.VMEM((r, nb, 2 * n), BF16)],
        compiler_params=_cparams(1),
        name="ssm_sample",
    )(zbv, s0_re, s0_im, bmat, cmat, ab, d, gw, gb)
    return yb.reshape(nb * seq_len, c), s_re, s_im


def _prompt_front_kernel(n_blk, nr, x_ref, g_ref, w_ref, qg_ref, kg_ref, e_ref,
                         cw_ref, cb_ref, wg_ref, bg_ref, lam_ref,
                         bmat_ref, cmat_ref, ab_ref, apow_ref, d_ref, gw_ref, gb_ref,
                         q_ref, k_ref, v_ref, gc_ref, kf_ref, vf_ref,
                         ya_ref, conv_ref, hl_ref, yb_ref, sre_ref, sim_ref,
                         hn_ref, za_ref, zb_ref,
                         xc_ref, a_ref, b_ref, cina_ref, cx_ref, ch_ref,
                         u_ref, g2_ref, xs_ref, xb_ref, cinb_ref, cst_ref):
    _lru_prompt_init(cx_ref, ch_ref)
    _ssm_prompt_init(cst_ref)
    _inproj_kernel(n_blk, x_ref, g_ref, w_ref, qg_ref, kg_ref, e_ref,
                   za_ref, zb_ref, q_ref, k_ref, v_ref, gc_ref, kf_ref, vf_ref, hn_ref)
    _lru_prompt_main(n_blk, nr, za_ref, cw_ref, cb_ref, wg_ref, bg_ref, lam_ref, ya_ref,
                     xc_ref, a_ref, b_ref, cina_ref, cx_ref, ch_ref)
    _ssm_prompt_main(n_blk, nr, zb_ref, bmat_ref, cmat_ref, ab_ref, apow_ref, d_ref, gw_ref, gb_ref,
                     yb_ref, u_ref, g2_ref, xs_ref, xb_ref, cinb_ref, cst_ref)
    _lru_prompt_finish(n_blk, nr, za_ref, conv_ref, hl_ref, ch_ref)
    _ssm_prompt_finish(sre_ref, sim_ref, cst_ref)


def _prompt_front(x3, keep_rows, norm_g, w_in_bf, qg, kg, emat, cw, cb, wg, bg, lam,
                  bmat, cmat, ab, apow, d, gw, gb):
    r, nr, n = CHUNK, CHUNK_ROWS, N_STATE
    nb, nrows, _ = x3.shape
    assert nrows % nr == 0 and keep_rows % nr == 0
    skip = (nrows - keep_rows) // nr
    kv_spec = pl.BlockSpec((None, nr, r * D_ATT), lambda b, t: (b, jnp.maximum(t - skip, 0), 0))

    def per_seq(rows, width):
        return pl.BlockSpec((None, rows, width), lambda b, t: (b, 0, 0))

    def out(c, dtype=F32):
        return jax.ShapeDtypeStruct((nb, nrows, r * c), dtype)

    def seq_out(rows, width):
        return jax.ShapeDtypeStruct((nb, rows, width), F32)

    kv_shape = jax.ShapeDtypeStruct((nb, keep_rows, r * D_ATT), F32)
    att = _tile3(nr, r * D_ATT)
    res = pl.pallas_call(
        functools.partial(_prompt_front_kernel, r, nr),
        grid=(nb, nrows // nr),
        in_specs=[_tile3(nr, r * D_MODEL), _full((1, D_MODEL)), _full((D_MODEL, D_IN)),
                  _full((1, D_ATT)), _full((1, D_ATT)), _full((D_ATT, D_ATT)),
                  _full((CONV_W, D_LRU)), _full((1, D_LRU)), _full((D_LRU, 2 * D_LRU)),
                  _full((1, 2 * D_LRU)), _full((1, D_LRU)),
                  _full((D_SSM, 2 * n)), _full((2 * n, D_SSM)), _full((2, n)), _full((r, 2 * n)),
                  _full((1, D_SSM)), _full((D_SSM, D_SSM)), _full((1, D_SSM))],
        out_specs=[att, att, att, att, kv_spec, kv_spec,
                   _tile3(nr, r * D_LRU), per_seq(CONV_W - 1, D_LRU), per_seq(1, D_LRU),
                   _tile3(nr, r * D_SSM), per_seq(1, n), per_seq(1, n)],
        out_shape=[out(D_ATT, BF16), out(D_ATT, BF16), out(D_ATT, BF16), out(D_ATT), kv_shape, kv_shape,
                   out(D_LRU), seq_out(CONV_W - 1, D_LRU), seq_out(1, D_LRU),
                   out(D_SSM), seq_out(1, n), seq_out(1, n)],
        scratch_shapes=[pltpu.VMEM((r * nr, D_MODEL), BF16),
                        pltpu.VMEM((nr, r * 2 * D_LRU), F32), pltpu.VMEM((nr, r * 2 * D_SSM), F32),
                        pltpu.VMEM((r, nr, D_LRU), F32), pltpu.VMEM((r, nr, D_LRU), F32),
                        pltpu.VMEM((r, nr, D_LRU), F32), pltpu.VMEM((nr, D_LRU), F32),
                        pltpu.VMEM((8, D_LRU), F32), pltpu.VMEM((8, D_LRU), F32),
                        pltpu.VMEM((r, nr, D_SSM), F32), pltpu.VMEM((r, nr, D_SSM), F32),
                        pltpu.VMEM((r, nr, 2 * n), F32), pltpu.VMEM((r, nr, 2 * n), BF16),
                        pltpu.VMEM((nr, 2 * n), F32), pltpu.VMEM((8, 2 * n), F32)],
        compiler_params=_cparams(2),
        name="prompt_front",
    )(x3, norm_g, w_in_bf, qg, kg, emat, cw, cb, wg, bg, lam, bmat, cmat, ab, apow, d, gw, gb)
    q, k, v, gc, k_rows, v_rows, ya, conv_new, h_last, yb, s_re, s_im = res
    return (q, k, v, gc, k_rows, v_rows, ya, conv_new, h_last.reshape(nb, D_LRU),
            yb, s_re.reshape(nb, n), s_im.reshape(nb, n))


def _attend(q, kk, vv, bias_ref, first):
    blk = q.shape[0]
    lane = lax.broadcasted_iota(jnp.int32, (1, LANES), 1)
    outs, lses = [], []
    for pair in range(N_HEADS // 2):
        sl = slice(pair * LANES, (pair + 1) * LANES)
        qp, kp, vp = q[:, sl], kk[:, sl], vv[:, sl]
        acc = jnp.zeros((blk, LANES), F32)
        lse = jnp.zeros((blk, LANES), F32)
        for hh in range(2):
            head_lanes = (lane >= HEAD_DIM) if hh else (lane < HEAD_DIM)
            qh = jnp.where(head_lanes, qp, jnp.zeros_like(qp))
            s = lax.dot_general(qh, kp, (((1,), (1,)), ((), ())), preferred_element_type=F32)
            s = s + bias_ref[2 * pair + hh + N_HEADS * first]
            m = jnp.max(s, axis=-1, keepdims=True)
            e = jnp.exp2(s - m)
            den = jnp.sum(e, axis=-1, keepdims=True)
            vh = jnp.where(head_lanes, vp, jnp.zeros_like(vp))
            pv = jnp.dot(e.astype(BF16), vh, preferred_element_type=F32)
            acc = acc + pv * (1.0 / den)
            lse = jnp.where(head_lanes, m + jnp.log2(den), lse)
        outs.append(acc)
        lses.append(lse)
    return outs, lses


SUB_ROWS = ATT_BLOCK // (CHUNK // 4)
D1_ROWS = 2 * ATT_BLOCK // CHUNK
HALF_ROWS = D1_ROWS // 2
D16_UNROLL = 4
SEQ_PER_STEP = CHUNK // D16_UNROLL + 3 * (ATT_BLOCK // SUB_ROWS)


def _put_pairs(dst, lead, rows, vals, src_rows=None):
    for pair in range(N_HEADS // 2):
        v = vals[pair] if src_rows is None else vals[pair][src_rows]
        dst[lead, rows, pair * LANES:(pair + 1) * LANES] = v


def _cache_copy(hbm, buf, sem, layer, seq, slot, which):
    return pltpu.make_async_copy(hbm.at[layer, seq], buf.at[slot], sem.at[which, slot])


def _attn_kernel(layer, n_seq, t_new,
                 q_ref, k_ref, v_ref, b16_ref, b4_ref, b1_ref,
                 qs_ref, kn_ref, vn_ref, kt_hbm, vt_hbm, sbias_ref,
                 o_ref, os_ref,
                 qst, kst, vst, o16, l16, o4, l4, o1, l1, qq, kk, vv, qf, kf, vf, qq1, kk1, vv1,
                 kbuf, vbuf, sem):
    blk, c = ATT_BLOCK, D_ATT
    jb = pl.program_id(1)
    step = pl.program_id(0) * pl.num_programs(1) + jb
    n_steps = pl.num_programs(0) * pl.num_programs(1)
    first = jnp.where(jb == 0, 1, 0)

    @pl.when(step == 0)
    def _():
        _cache_copy(kt_hbm, kbuf, sem, layer, 0, 0, 0).start()
        _cache_copy(vt_hbm, vbuf, sem, layer, 0, 0, 1).start()

    @pl.when(jb == 0)
    def _():
        for s in range(CHUNK):
            kst[s, 0:blk, :] = jnp.zeros((blk, c), BF16)
            vst[s, 0:blk, :] = jnp.zeros((blk, c), BF16)

    for s in range(CHUNK):
        sl = slice(s * c, (s + 1) * c)
        qst[s] = q_ref[:, sl]
        kst[s, blk:2 * blk, :] = k_ref[:, sl]
        vst[s, blk:2 * blk, :] = v_ref[:, sl]

    rows_qh = N_HEADS * t_new
    row_head = lax.broadcasted_iota(jnp.int32, (rows_qh, c), 0) // t_new
    lane_head = lax.broadcasted_iota(jnp.int32, (rows_qh, c), 1) // HEAD_DIM
    own = row_head == lane_head
    nt = (((1,), (1,)), ((), ()))

    def sample_seq(local):
        g = step * SEQ_PER_STEP + local
        slot = lax.rem(g, 2)
        nxt = jnp.minimum(g + 1, n_seq - 1)
        _cache_copy(kt_hbm, kbuf, sem, layer, nxt, 1 - slot, 0).start()
        _cache_copy(vt_hbm, vbuf, sem, layer, nxt, 1 - slot, 1).start()
        _cache_copy(kt_hbm, kbuf, sem, layer, g, slot, 0).wait()
        _cache_copy(vt_hbm, vbuf, sem, layer, g, slot, 1).wait()
        kt = kbuf[slot].reshape(c, PAST_LEN).astype(BF16)
        vt = vbuf[slot].reshape(c, PAST_LEN).astype(BF16)
        pad = jnp.zeros((LANES - t_new, c), F32)
        kn = jnp.concatenate([kn_ref[local], pad], axis=0).astype(BF16)
        vn = jnp.concatenate([vn_ref[local], pad], axis=0).astype(BF16)
        q8 = qs_ref[local].astype(F32)
        qbd = jnp.where(own, jnp.concatenate([q8] * N_HEADS, axis=0), 0.0).astype(BF16)
        s = jnp.concatenate([jnp.dot(qbd, kt, preferred_element_type=F32),
                             lax.dot_general(qbd, kn, nt, preferred_element_type=F32)], axis=1)
        es, dens, lses = [], [], []
        for p, (window, _) in enumerate(DILATED):
            lo = PAST_LEN - window
            sp = s[:, lo:] + sbias_ref[p, :, lo:]
            m = jnp.max(sp, axis=-1, keepdims=True)
            e = jnp.exp2(sp - m)
            den = jnp.sum(e, axis=-1, keepdims=True)
            es.append(e)
            dens.append(den)
            lses.append(m + jnp.log2(den))
        top = jnp.maximum(jnp.maximum(lses[0], lses[1]), lses[2])
        ws = [jnp.exp2(l - top) for l in lses]
        wsum = ws[0] + ws[1] + ws[2]
        cs = [w / (den * wsum) for w, den in zip(ws, dens)]
        cut1, cut2 = PAST_LEN - DILATED[1][0], PAST_LEN - DILATED[0][0]
        far = cs[2] * es[2][:, :cut1]
        mid = cs[2] * es[2][:, cut1:cut2] + cs[1] * es[1][:, :cut2 - cut1]
        near = cs[2] * es[2][:, cut2:] + cs[1] * es[1][:, cut2 - cut1:] + cs[0] * es[0]
        pc = jnp.concatenate([far, mid, near], axis=1).astype(BF16)
        o_all = (lax.dot_general(pc[:, :PAST_LEN], vt, nt, preferred_element_type=F32)
                 + jnp.dot(pc[:, PAST_LEN:], vn, preferred_element_type=F32))
        o_all = jnp.where(own, o_all, 0.0)
        out = o_all[0:t_new]
        for h in range(1, N_HEADS):
            out = out + o_all[h * t_new:(h + 1) * t_new]
        os_ref[local] = out

    def d16_body(i, carry):
        for u in range(D16_UNROLL):
            s = i * D16_UNROLL + u
            outs, lses = _attend(qst.at[s], kst.at[s], vst.at[s], b16_ref, first)
            _put_pairs(o16, s, slice(None), outs)
            _put_pairs(l16, s, slice(None), lses)
        sample_seq(i)
        return carry

    n_d16 = CHUNK // D16_UNROLL
    lax.fori_loop(0, n_d16, d16_body, 0)

    def sub_body(j, carry):
        r0 = pl.multiple_of(j * SUB_ROWS, SUB_ROWS)
        first_j = first * jnp.where(j == 0, 1, 0)
        seq0 = n_d16 + 3 * j

        for r in range(4):
            for m in range(CHUNK // 4):
                s = r + 4 * m
                qq[m * SUB_ROWS:(m + 1) * SUB_ROWS, :] = qst[s, pl.ds(r0, SUB_ROWS), :]
                kk[2 * m * SUB_ROWS:2 * (m + 1) * SUB_ROWS, :] = kst[s, pl.ds(blk - SUB_ROWS + r0, 2 * SUB_ROWS), :]
                vv[2 * m * SUB_ROWS:2 * (m + 1) * SUB_ROWS, :] = vst[s, pl.ds(blk - SUB_ROWS + r0, 2 * SUB_ROWS), :]
            outs, lses = _attend(qq, kk, vv, b4_ref, first_j)
            for m in range(CHUNK // 4):
                _put_pairs(o4, r + 4 * m, slice(None), outs, slice(m * SUB_ROWS, (m + 1) * SUB_ROWS))
                _put_pairs(l4, r + 4 * m, slice(None), lses, slice(m * SUB_ROWS, (m + 1) * SUB_ROWS))
        sample_seq(seq0)

        for t2 in range(SUB_ROWS // D1_ROWS):
            rt = r0 + t2 * D1_ROWS
            for s in range(CHUNK):
                qp = qst[s, pl.ds(pl.multiple_of(rt, D1_ROWS), D1_ROWS), :].astype(F32)
                for part in range(2):
                    qf[part * blk + s * HALF_ROWS:part * blk + (s + 1) * HALF_ROWS, :] = (
                        qp[part * HALF_ROWS:(part + 1) * HALF_ROWS])
                for src, dst in ((kst, kf), (vst, vf)):
                    kp = src[s, pl.ds(pl.multiple_of(blk - D1_ROWS + rt, D1_ROWS), 2 * D1_ROWS), :].astype(F32)
                    for part in range(3):
                        dst[part * blk + s * HALF_ROWS:part * blk + (s + 1) * HALF_ROWS, :] = (
                            kp[(part + 1) * HALF_ROWS:(part + 2) * HALF_ROWS])
            qq1[...] = qf[...].astype(BF16)
            kk1[...] = kf[...].astype(BF16)
            vv1[...] = vf[...].astype(BF16)
            for part in range(2):
                sel = first_j if (t2 == 0 and part == 0) else 0
                outs, lses = _attend(qq1.at[pl.ds(part * blk, blk), :], kk1.at[pl.ds(part * blk, 2 * blk), :],
                                     vv1.at[pl.ds(part * blk, 2 * blk), :], b1_ref, sel)
                rows = slice(t2 * D1_ROWS + part * HALF_ROWS, t2 * D1_ROWS + (part + 1) * HALF_ROWS)
                for s in range(CHUNK):
                    _put_pairs(o1, s, rows, outs, slice(s * HALF_ROWS, (s + 1) * HALF_ROWS))
                    _put_pairs(l1, s, rows, lses, slice(s * HALF_ROWS, (s + 1) * HALF_ROWS))
        sample_seq(seq0 + 1)

        for s in range(CHUNK):
            for lo in range(0, c, LANES):
                ls = [l16[s, pl.ds(r0, SUB_ROWS), lo:lo + LANES], l4[s, :, lo:lo + LANES], l1[s, :, lo:lo + LANES]]
                os = [o16[s, pl.ds(r0, SUB_ROWS), lo:lo + LANES], o4[s, :, lo:lo + LANES], o1[s, :, lo:lo + LANES]]
                top = jnp.maximum(jnp.maximum(ls[0], ls[1]), ls[2])
                ws = [jnp.exp2(l - top) for l in ls]
                mix = (ws[0] * os[0] + ws[1] * os[1] + ws[2] * os[2]) / (ws[0] + ws[1] + ws[2])
                o_ref[pl.ds(r0, SUB_ROWS), s * c + lo:s * c + lo + LANES] = mix
        sample_seq(seq0 + 2)
        return carry

    lax.fori_loop(0, blk // SUB_ROWS, sub_body, 0)

    for s in range(CHUNK):
        for lo in range(0, c, LANES):
            kst[s, 0:blk, lo:lo + LANES] = kst[s, blk:2 * blk, lo:lo + LANES]
            vst[s, 0:blk, lo:lo + LANES] = vst[s, blk:2 * blk, lo:lo + LANES]

    @pl.when(step == n_steps - 1)
    def _():
        spare = 1 - (n_seq - 1) % 2
        _cache_copy(kt_hbm, kbuf, sem, layer, n_seq - 1, spare, 0).wait()
        _cache_copy(vt_hbm, vbuf, sem, layer, n_seq - 1, spare, 1).wait()


def _prompt_bias(q_idx, k_idx, dil):
    dist = q_idx[:, None] - k_idx[None, :]
    valid = (dist >= 0) & (dist <= ATT_J)
    bias = -(ALIBI_SLOPES * LOG2_E)[:, None, None] * (dist * dil).astype(np.float32)[None]
    table = np.where(valid[None], bias, NEG_INF)
    masked = np.where((k_idx >= 0)[None, None, :], table, NEG_INF)
    return jnp.asarray(np.concatenate([table, masked], axis=0), dtype=F32)


def _attention(q, k, v, qs, k_new, v_new, cache_kt, cache_vt, layer):
    blk, c = ATT_BLOCK, D_ATT
    nb, nrows, width = q.shape
    n_seq = cache_kt.shape[1]
    t_new = qs.size // (n_seq * c)
    n_steps = nb * (nrows // blk)
    assert nrows % blk == 0 and n_seq == n_steps * SEQ_PER_STEP and cache_kt.shape[-1] == PAST_LEN
    a = np.arange(blk)
    c2 = np.arange(2 * blk)
    b16 = _prompt_bias(a, c2 - blk, 16)
    k4 = 4 * (c2 % SUB_ROWS + SUB_ROWS * ((c2 // SUB_ROWS) % 2 - 1)) + c2 // (2 * SUB_ROWS)
    b4 = _prompt_bias(4 * (a % SUB_ROWS) + a // SUB_ROWS, k4, 4)
    order1 = CHUNK * (a % HALF_ROWS) + a // HALF_ROWS
    b1 = _prompt_bias(order1, np.concatenate([order1 - blk, order1]), 1)
    spec = _tile3(blk, width)
    bias_spec = _full((2 * N_HEADS, blk, 2 * blk))
    seq_spec = pl.BlockSpec((SEQ_PER_STEP, t_new, c), lambda b, t: (b * (nrows // blk) + t, 0, 0))
    hbm_spec = pl.BlockSpec(memory_space=pl.ANY)

    def st(rows, dtype):
        return pltpu.VMEM((CHUNK, rows, c), dtype)

    o, o_s = pl.pallas_call(
        functools.partial(_attn_kernel, layer, n_seq, t_new),
        grid=(nb, nrows // blk),
        in_specs=[spec, spec, spec, bias_spec, bias_spec, bias_spec,
                  seq_spec, seq_spec, seq_spec, hbm_spec, hbm_spec,
                  _full((len(DILATED), N_HEADS * t_new, PAST_LEN + LANES))],
        out_specs=[spec, seq_spec],
        out_shape=[jax.ShapeDtypeStruct((nb, nrows, width), F32),
                   jax.ShapeDtypeStruct((n_seq, t_new, c), F32)],
        scratch_shapes=[st(blk, BF16), st(2 * blk, BF16), st(2 * blk, BF16),
                        st(blk, F32), st(blk, F32),
                        st(SUB_ROWS, F32), st(SUB_ROWS, F32), st(SUB_ROWS, F32), st(SUB_ROWS, F32),
                        pltpu.VMEM((blk, c), BF16), pltpu.VMEM((2 * blk, c), BF16), pltpu.VMEM((2 * blk, c), BF16),
                        pltpu.VMEM((2 * blk, c), F32), pltpu.VMEM((3 * blk, c), F32), pltpu.VMEM((3 * blk, c), F32),
                        pltpu.VMEM((2 * blk, c), BF16), pltpu.VMEM((3 * blk, c), BF16), pltpu.VMEM((3 * blk, c), BF16),
                        pltpu.VMEM((2, N_HEADS, HEAD_DIM, PAST_LEN), F32),
                        pltpu.VMEM((2, N_HEADS, HEAD_DIM, PAST_LEN), F32),
                        pltpu.SemaphoreType.DMA((2, 2))],
        compiler_params=_cparams(2), name="attention",
    )(q, k, v, b16, b4, b1,
      qs.reshape(n_seq, t_new, c), k_new.reshape(n_seq, t_new, c), v_new.reshape(n_seq, t_new, c),
      cache_kt, cache_vt, _sample_bias(t_new))
    return o, o_s.reshape(n_seq * t_new, c)


def _sample_bias(t_new):
    col = np.arange(PAST_LEN + LANES)
    real = col < PAST_LEN + t_new
    t = np.arange(t_new)
    dist = PAST_LEN + t[:, None] - col[None, :]
    out = []
    for window, dil in DILATED:
        valid = real[None, :] & (dist >= 0) & (dist % dil == 0) & (dist <= window)
        bias = -(ALIBI_SLOPES * LOG2_E)[:, None, None] * dist.astype(np.float32)[None]
        bias = np.where(valid[None], bias, NEG_INF)
        out.append(bias.reshape(N_HEADS * t_new, PAST_LEN + LANES))
    return jnp.asarray(np.stack(out, axis=0), dtype=F32)


def _outproj_kernel(n_blk, x_ref, ya_ref, yb_ref, gc_ref, o_ref, og_ref, w_ref, out_ref, yn_ref):
    rows = x_ref.shape[0]
    a_hi, b_hi = D_LRU, D_LRU + D_SSM

    def norm_into(y, lo, hi, s):
        yn = y * lax.rsqrt(jnp.mean(y * y, axis=-1, keepdims=True) + EPS) * og_ref[:, lo:hi]
        yn_ref[s * rows:(s + 1) * rows, lo:hi] = yn.astype(BF16)

    for s in range(n_blk):
        att = slice(s * D_ATT, (s + 1) * D_ATT)
        yc = o_ref[:, att] * _silu(gc_ref[:, att])
        norm_into(ya_ref[:, s * D_LRU:(s + 1) * D_LRU], 0, a_hi, s)
        norm_into(yb_ref[:, s * D_SSM:(s + 1) * D_SSM], a_hi, b_hi, s)
        norm_into(yc, b_hi, D_MODEL, s)
    res = jnp.dot(yn_ref[...], w_ref[...], preferred_element_type=F32)
    for s in range(n_blk):
        sl = slice(s * D_MODEL, (s + 1) * D_MODEL)
        out_ref[:, sl] = x_ref[:, sl] + res[s * rows:(s + 1) * rows]


def _outproj(x3, n_blk, rows_t, ya, yb, gc, o, og, w_out_bf):
    nb, nrows, _ = x3.shape
    att_spec = _tile3(rows_t, n_blk * D_ATT)
    return pl.pallas_call(
        functools.partial(_outproj_kernel, n_blk),
        grid=(nb, nrows // rows_t),
        in_specs=[_tile3(rows_t, n_blk * D_MODEL), _tile3(rows_t, n_blk * D_LRU),
                  _tile3(rows_t, n_blk * D_SSM), att_spec, att_spec,
                  _full((1, D_MODEL)), _full((D_MODEL, D_MODEL))],
        out_specs=_tile3(rows_t, n_blk * D_MODEL),
        out_shape=jax.ShapeDtypeStruct(x3.shape, F32),
        scratch_shapes=[pltpu.VMEM((n_blk * rows_t, D_MODEL), BF16)],
        compiler_params=_cparams(2),
        name="outproj",
    )(x3, ya, yb, gc, o, og, w_out_bf)


def _block_diag(blocks):
    k, i, j = blocks.shape
    eye = jnp.eye(k, dtype=blocks.dtype)
    return jnp.einsum("kij,kl->kilj", blocks, eye).reshape(k * i, k * j)


def kernel(x_prompt, x_sample, state_conv, state_lru, state_ssm_re, state_ssm_im, cache_k, cache_v,
           norm_g, w_in, conv_w, conv_b, w_r, b_r, w_i, b_i, lru_lambda,
           ssm_lambda_re, ssm_lambda_im, ssm_log_dt, ssm_b_re, ssm_b_im, ssm_c_re, ssm_c_im,
           ssm_d, glu_w, glu_b, q_norm_g, k_norm_g, out_norm_g, w_out):
    bp, seq, _ = x_prompt.shape
    bs_, t_new, _ = x_sample.shape
    keep = min(PAST_LEN, seq)
    assert seq % (CHUNK * PROJ_ROWS) == 0 and keep % (CHUNK * PROJ_ROWS) == 0
    n_s = bs_ * t_new
    rows_s = min(ROW_TILE, n_s)
    xp = x_prompt.reshape(bp, seq // CHUNK, CHUNK * D_MODEL)
    xs = x_sample.reshape(1, n_s, D_MODEL)
    cache_kt = jnp.transpose(cache_k, (0, 1, 3, 4, 2))
    cache_vt = jnp.transpose(cache_v, (0, 1, 3, 4, 2))

    head_of = jnp.arange(D_ATT) // HEAD_DIM
    emat = (head_of[:, None] == head_of[None, :]).astype(BF16)

    def chan_major(b):
        return jnp.transpose(b, (0, 3, 1, 2)).reshape(DEPTH, SSM_GROUP, N_STATE)

    def vec(a):
        return a.reshape(DEPTH, 1, N_STATE)

    ab_all, bbr_all, bbi_all, apow_all = _ssm_discretise(
        vec(ssm_lambda_re), vec(ssm_lambda_im), vec(ssm_log_dt), chan_major(ssm_b_re), chan_major(ssm_b_im))

    sp_list, ss_list = [], []
    for l in range(DEPTH):
        w_in_bf = w_in[l].astype(BF16)
        w_out_bf = w_out[l].astype(BF16)
        ng = norm_g[l].reshape(1, D_MODEL)
        qg = jnp.tile(q_norm_g[l], N_HEADS).reshape(1, D_ATT)
        kg = jnp.tile(k_norm_g[l], N_HEADS).reshape(1, D_ATT)
        og = out_norm_g[l].reshape(1, D_MODEL)
        cw, cb = conv_w[l], conv_b[l].reshape(1, D_LRU)
        wg = jnp.concatenate([_block_diag(w_r[l]), _block_diag(w_i[l])], axis=1).astype(BF16)
        bg = jnp.concatenate([b_r[l], b_i[l]]).reshape(1, 2 * D_LRU)
        lam = lru_lambda[l].reshape(1, D_LRU)

        def b_matrix(bb):
            blocks = jnp.transpose(bb.reshape(SSM_GROUP, SSM_GROUPS, SSM_STATE), (1, 0, 2))
            return _block_diag(blocks)
        bmat = jnp.concatenate([b_matrix(bbr_all[l]), b_matrix(bbi_all[l])], axis=1).astype(BF16)
        cmat = jnp.concatenate([_block_diag(jnp.transpose(ssm_c_re[l], (0, 2, 1))),
                                -_block_diag(jnp.transpose(ssm_c_im[l], (0, 2, 1)))], axis=0).astype(BF16)
        d = ssm_d[l].reshape(1, D_SSM)
        gw = glu_w[l].astype(BF16)
        gb = glu_b[l].reshape(1, D_SSM)

        q, k, v, gc, k_rows, v_rows, ya, conv_new, h_last, yb, s_re, s_im = _prompt_front(
            xp, keep // CHUNK, ng, w_in_bf, qg, kg, emat, cw, cb, wg, bg, lam,
            bmat, cmat, ab_all[l], apow_all[l], d, gw, gb)
        sp_list.append((conv_new, h_last,
                        s_re.reshape(bp, SSM_GROUPS, SSM_STATE), s_im.reshape(bp, SSM_GROUPS, SSM_STATE),
                        k_rows.reshape(bp, keep, N_HEADS, HEAD_DIM), v_rows.reshape(bp, keep, N_HEADS, HEAD_DIM)))

        zas, zbs, qs, _, _, gcs, ks_rows, vs_rows = _inproj(xs, 1, rows_s, n_s, ng, w_in_bf, qg, kg, emat)
        yas, conv_new, h_last = _lru_sample(zas, bs_, t_new, state_conv[l], state_lru[l], cw, cb, wg, bg, lam)
        ybs, s_re, s_im = _ssm_sample(zbs, bs_, t_new, state_ssm_re[l].reshape(bs_, N_STATE),
                                      state_ssm_im[l].reshape(bs_, N_STATE), bmat, cmat, ab_all[l], d, gw, gb)
        ss_list.append((conv_new, h_last,
                        s_re.reshape(bs_, SSM_GROUPS, SSM_STATE), s_im.reshape(bs_, SSM_GROUPS, SSM_STATE),
                        ks_rows.reshape(bs_, t_new, N_HEADS, HEAD_DIM), vs_rows.reshape(bs_, t_new, N_HEADS, HEAD_DIM)))

        o, o_s = _attention(q, k, v, qs, ks_rows, vs_rows, cache_kt, cache_vt, l)
        xp = _outproj(xp, CHUNK, PROJ_ROWS, ya, yb, gc, o, og, w_out_bf)
        xs = _outproj(xs, 1, rows_s, yas.reshape(1, n_s, D_LRU), ybs.reshape(1, n_s, D_SSM), gcs,
                      o_s.reshape(1, n_s, D_ATT), og, w_out_bf)

    def stack(lst, i):
        return jnp.stack([s[i] for s in lst], axis=0)

    return (xp.reshape(bp, seq, D_MODEL), xs.reshape(bs_, t_new, D_MODEL),
            stack(sp_list, 0), stack(sp_list, 1), stack(sp_list, 2), stack(sp_list, 3),
            stack(sp_list, 4), stack(sp_list, 5),
            stack(ss_list, 0), stack(ss_list, 1), stack(ss_list, 2), stack(ss_list, 3),
            stack(ss_list, 4), stack(ss_list, 5))
```

```python
import functools

import numpy as np
import jax
import jax.numpy as jnp
from jax import lax
from jax.experimental import pallas as pl
from jax.experimental.pallas import tpu as pltpu

F32 = jnp.float32
BF16 = jnp.bfloat16

D_MODEL = 1024
DEPTH = 4
PAST_LEN = 2048
D_LRU = 384
CONV_W = 4
RG_C = 8.0
D_SSM = 256
SSM_GROUP = 16
SSM_GROUPS = 16
SSM_STATE = 64
N_STATE = SSM_GROUPS * SSM_STATE
D_ATT = 384
HEAD_DIM = 64
N_HEADS = 6
DILATED = ((128, 1), (512, 4), (2048, 16))
ATT_J = 128
D_IN = 2 * D_LRU + 2 * D_SSM + 4 * D_ATT
EPS = 1e-6
NEG_INF = -1e30
LOG2_E = float(np.log2(np.e))
ALIBI_SLOPES = np.exp2(-8.0 * np.arange(1, N_HEADS + 1, dtype=np.float32) / N_HEADS).astype(np.float32)

C_ZA = (0, 2 * D_LRU)
C_ZB = (2 * D_LRU, 2 * D_LRU + 2 * D_SSM)
C_Q = (C_ZB[1], C_ZB[1] + D_ATT)
C_K = (C_Q[1], C_Q[1] + D_ATT)
C_V = (C_K[1], C_K[1] + D_ATT)
C_GC = (C_V[1], C_V[1] + D_ATT)

V7X_VMEM_LIMIT = 56 * 1024 * 1024
LANES = 128
CHUNK = 16
CHUNK_ROWS = 32
PROJ_ROWS = 64
ROW_TILE = 512
ATT_BLOCK = ATT_J


def _cparams(n_axes):
    return pltpu.CompilerParams(dimension_semantics=("arbitrary",) * n_axes,
                                vmem_limit_bytes=V7X_VMEM_LIMIT)


def _sigmoid(x):
    return 1.0 / (1.0 + jnp.exp(-x))


def _silu(x):
    return x * _sigmoid(x)


def _softplus(x):
    return jnp.maximum(x, 0.0) + jnp.log1p(jnp.exp(-jnp.abs(x)))


def _full(shape):
    nd = len(shape)
    return pl.BlockSpec(shape, lambda *_: (0,) * nd)


def _tile3(rows, width):
    return pl.BlockSpec((None, rows, width), lambda b, t: (b, t, 0))


def _inproj_kernel(n_blk, x_ref, g_ref, w_ref, qg_ref, kg_ref, e_ref,
                   za_ref, zb_ref, q_ref, k_ref, v_ref, gc_ref, kf_ref, vf_ref, hn_ref):
    rows = x_ref.shape[0]
    for s in range(n_blk):
        x = x_ref[:, s * D_MODEL:(s + 1) * D_MODEL]
        ms = jnp.mean(x * x, axis=-1, keepdims=True)
        hn_ref[s * rows:(s + 1) * rows, :] = (x * lax.rsqrt(ms + EPS) * g_ref[...]).astype(BF16)
    hn = hn_ref[...]

    def proj(cols):
        return jnp.dot(hn, w_ref[:, cols[0]:cols[1]], preferred_element_type=F32)

    def head_norm(z, gain):
        ss = jnp.dot((z * z).astype(BF16), e_ref[...], preferred_element_type=F32)
        return z * lax.rsqrt(ss * (1.0 / HEAD_DIM) + EPS) * gain

    def put(ref, z):
        c = z.shape[1]
        for s in range(n_blk):
            ref[:, s * c:(s + 1) * c] = z[s * rows:(s + 1) * rows, :].astype(ref.dtype)

    put(za_ref, proj(C_ZA))
    put(zb_ref, proj(C_ZB))
    q = head_norm(proj(C_Q), qg_ref[...]) * (HEAD_DIM ** -0.5 * LOG2_E)
    k = head_norm(proj(C_K), kg_ref[...])
    v = proj(C_V)
    put(q_ref, q)
    put(k_ref, k)
    put(v_ref, v)
    put(kf_ref, k)
    put(vf_ref, v)
    put(gc_ref, proj(C_GC))


def _inproj(x3, n_blk, rows_t, keep_rows, norm_g, w_in_bf, qg, kg, emat):
    nb, nrows, _ = x3.shape
    assert nrows % rows_t == 0 and keep_rows % rows_t == 0
    skip = (nrows - keep_rows) // rows_t
    kv_spec = pl.BlockSpec((None, rows_t, n_blk * D_ATT), lambda b, t: (b, jnp.maximum(t - skip, 0), 0))

    def out(c, dtype=F32):
        return jax.ShapeDtypeStruct((nb, nrows, n_blk * c), dtype)

    kv_shape = jax.ShapeDtypeStruct((nb, keep_rows, n_blk * D_ATT), F32)
    return pl.pallas_call(
        functools.partial(_inproj_kernel, n_blk),
        grid=(nb, nrows // rows_t),
        in_specs=[_tile3(rows_t, n_blk * D_MODEL), _full((1, D_MODEL)), _full((D_MODEL, D_IN)),
                  _full((1, D_ATT)), _full((1, D_ATT)), _full((D_ATT, D_ATT))],
        out_specs=[_tile3(rows_t, n_blk * 2 * D_LRU), _tile3(rows_t, n_blk * 2 * D_SSM),
                   _tile3(rows_t, n_blk * D_ATT), _tile3(rows_t, n_blk * D_ATT),
                   _tile3(rows_t, n_blk * D_ATT), _tile3(rows_t, n_blk * D_ATT), kv_spec, kv_spec],
        out_shape=[out(2 * D_LRU), out(2 * D_SSM), out(D_ATT, BF16), out(D_ATT, BF16),
                   out(D_ATT, BF16), out(D_ATT), kv_shape, kv_shape],
        scratch_shapes=[pltpu.VMEM((n_blk * rows_t, D_MODEL), BF16)],
        compiler_params=_cparams(2),
        name="inproj",
    )(x3, norm_g, w_in_bf, qg, kg, emat)


def _lru_conv(xa, prev, cw_ref, cb_ref, xc_ref, r_steps):
    for s in range(r_steps):
        acc = cb_ref[...] + cw_ref[3:4, :] * xa(s)
        for back in (1, 2, 3):
            src = xa(s - back) if s - back >= 0 else prev[back - s]
            acc = acc + cw_ref[3 - back:4 - back, :] * src
        xc_ref[s] = acc


def _lru_gates(xc, wg_ref, bg_ref, lam_ref):
    g = jnp.dot(xc.astype(BF16), wg_ref[...], preferred_element_type=F32) + bg_ref[...]
    r = _sigmoid(g[:, :D_LRU])
    i = _sigmoid(g[:, D_LRU:])
    log_a = -RG_C * r * _softplus(-lam_ref[...])
    a = jnp.exp(log_a)
    mult = jnp.sqrt(-jnp.tanh(log_a) * (a * a + 1.0))
    return a, mult, i


def _lru_prompt_init(cx_ref, ch_ref):
    @pl.when(pl.program_id(1) == 0)
    def _():
        cx_ref[...] = jnp.zeros_like(cx_ref)
        ch_ref[...] = jnp.zeros_like(ch_ref)


def _lru_prompt_finish(r_steps, nr, za_ref, conv_ref, hl_ref, ch_ref):
    c = D_LRU

    @pl.when(pl.program_id(1) == pl.num_programs(1) - 1)
    def _():
        hl_ref[...] = ch_ref[0:1, :]
        for j in range(CONV_W - 1):
            s = r_steps - (CONV_W - 1) + j
            conv_ref[j:j + 1, :] = za_ref[nr - 1:nr, s * 2 * c:s * 2 * c + c]


def _lru_prompt_main(r_steps, nr, za_ref, cw_ref, cb_ref, wg_ref, bg_ref, lam_ref, ya_ref,
                     xc_ref, a_ref, b_ref, cin_ref, cx_ref, ch_ref):
    c = D_LRU
    ti = pl.program_id(1)

    def xa(s):
        return za_ref[:, s * 2 * c:s * 2 * c + c]

    def ga(s):
        return za_ref[:, s * 2 * c + c:(s + 1) * 2 * c]

    row = lax.broadcasted_iota(jnp.int32, (nr, c), 0)
    prev = {j: jnp.where(row == 0, cx_ref[j - 1:j, :], pltpu.roll(xa(r_steps - j), 1, 0))
            for j in (1, 2, 3)}
    _lru_conv(xa, prev, cw_ref, cb_ref, xc_ref, r_steps)

    xc = xc_ref[...].reshape(r_steps * nr, c)
    a, mult, gate_i = _lru_gates(xc, wg_ref, bg_ref, lam_ref)
    flat_row = lax.broadcasted_iota(jnp.int32, (r_steps * nr, c), 0)
    mult = jnp.where(jnp.logical_and(flat_row == 0, ti == 0), 1.0, mult)
    a_ref[...] = a.reshape(r_steps, nr, c)
    b_ref[...] = (mult * gate_i * xc).reshape(r_steps, nr, c)

    h = jnp.zeros((nr, c), F32)
    p = jnp.ones((nr, c), F32)
    for s in range(r_steps):
        a_s = a_ref[s]
        h = a_s * h + b_ref[s]
        p = a_s * p
        b_ref[s] = h
        a_ref[s] = p

    carry = ch_ref[0:1, :]
    for i in range(nr):
        cin_ref[i:i + 1, :] = carry
        carry = a_ref[r_steps - 1, i:i + 1, :] * carry + b_ref[r_steps - 1, i:i + 1, :]
    ch_ref[0:1, :] = carry

    cin = cin_ref[...]
    for s in range(r_steps):
        h_s = b_ref[s] + a_ref[s] * cin
        ya_ref[:, s * c:(s + 1) * c] = h_s * _silu(ga(s))

    for j in (1, 2, 3):
        cx_ref[j - 1:j, :] = xa(r_steps - j)[nr - 1:nr, :]


def _lru_sample_kernel(r_steps, za_ref, sc_ref, h0_ref, cw_ref, cb_ref, wg_ref, bg_ref, lam_ref,
                       ya_ref, conv_ref, hl_ref, xc_ref, a_ref, b_ref):
    c = D_LRU
    nr = za_ref.shape[0]

    def xa(s):
        return za_ref[:, s * 2 * c:s * 2 * c + c]

    def ga(s):
        return za_ref[:, s * 2 * c + c:(s + 1) * 2 * c]

    prev = {j: sc_ref[:, (CONV_W - 1 - j) * c:(CONV_W - j) * c] for j in (1, 2, 3)}
    _lru_conv(xa, prev, cw_ref, cb_ref, xc_ref, r_steps)
    xc = xc_ref[...].reshape(r_steps * nr, c)
    a, mult, gate_i = _lru_gates(xc, wg_ref, bg_ref, lam_ref)
    a_ref[...] = a.reshape(r_steps, nr, c)
    b_ref[...] = (mult * gate_i * xc).reshape(r_steps, nr, c)
    h = h0_ref[...]
    for s in range(r_steps):
        h = a_ref[s] * h + b_ref[s]
        ya_ref[:, s * c:(s + 1) * c] = h * _silu(ga(s))
    hl_ref[...] = h
    for j in range(CONV_W - 1):
        conv_ref[:, j * c:(j + 1) * c] = xa(r_steps - (CONV_W - 1) + j)


def _lru_sample(za, nb, seq_len, state_conv, h0, cw, cb, wg, bg, lam):
    r, c = seq_len, D_LRU
    zav = za.reshape(nb, r * 2 * c)
    scv = state_conv.reshape(nb, (CONV_W - 1) * c)
    ya, conv_new, h_last = pl.pallas_call(
        functools.partial(_lru_sample_kernel, r),
        grid=(1,),
        in_specs=[_full((nb, r * 2 * c)), _full((nb, (CONV_W - 1) * c)), _full((nb, c)),
                  _full((CONV_W, c)), _full((1, c)), _full((c, 2 * c)), _full((1, 2 * c)),
                  _full((1, c))],
        out_specs=[_full((nb, r * c)), _full((nb, (CONV_W - 1) * c)), _full((nb, c))],
        out_shape=[jax.ShapeDtypeStruct((nb, r * c), F32),
                   jax.ShapeDtypeStruct((nb, (CONV_W - 1) * c), F32),
                   jax.ShapeDtypeStruct((nb, c), F32)],
        scratch_shapes=[pltpu.VMEM((r, nb, c), F32), pltpu.VMEM((r, nb, c), F32),
                        pltpu.VMEM((r, nb, c), F32)],
        compiler_params=_cparams(1),
        name="lru_sample",
    )(zav, scv, h0, cw, cb, wg, bg, lam)
    return ya.reshape(nb * seq_len, c), conv_new.reshape(nb, CONV_W - 1, c), h_last


def _ssm_disc_kernel(r_steps, lr_ref, li_ref, ldt_ref, br_ref, bi_ref,
                     ab_ref, bbr_ref, bbi_ref, apow_ref):
    lr, li = lr_ref[...], li_ref[...]
    dt = jnp.exp(ldt_ref[...])
    mag = jnp.exp(lr * dt)
    ang = li * dt
    ab_re, ab_im = mag * jnp.cos(ang), mag * jnp.sin(ang)
    den = lr * lr + li * li
    xr, yi = ab_re - 1.0, ab_im
    coef_re = (xr * lr + yi * li) / den
    coef_im = (yi * lr - xr * li) / den
    br, bi = br_ref[...], bi_ref[...]
    bbr_ref[...] = coef_re * br - coef_im * bi
    bbi_ref[...] = coef_re * bi + coef_im * br
    ab_ref[0:1, :] = ab_re
    ab_ref[1:2, :] = ab_im
    pr, pi = ab_re, ab_im
    for s in range(r_steps):
        apow_ref[s:s + 1, 0:N_STATE] = pr
        apow_ref[s:s + 1, N_STATE:2 * N_STATE] = pi
        pr, pi = pr * ab_re - pi * ab_im, pr * ab_im + pi * ab_re


def _ssm_discretise(lam_re, lam_im, log_dt, b_re, b_im):
    def vec():
        return pl.BlockSpec((None, 1, N_STATE), lambda l: (l, 0, 0))

    def mat():
        return pl.BlockSpec((None, SSM_GROUP, N_STATE), lambda l: (l, 0, 0))

    return pl.pallas_call(
        functools.partial(_ssm_disc_kernel, CHUNK),
        grid=(DEPTH,),
        in_specs=[vec(), vec(), vec(), mat(), mat()],
        out_specs=[pl.BlockSpec((None, 2, N_STATE), lambda l: (l, 0, 0)), mat(), mat(),
                   pl.BlockSpec((None, CHUNK, 2 * N_STATE), lambda l: (l, 0, 0))],
        out_shape=[jax.ShapeDtypeStruct((DEPTH, 2, N_STATE), F32),
                   jax.ShapeDtypeStruct((DEPTH, SSM_GROUP, N_STATE), F32),
                   jax.ShapeDtypeStruct((DEPTH, SSM_GROUP, N_STATE), F32),
                   jax.ShapeDtypeStruct((DEPTH, CHUNK, 2 * N_STATE), F32)],
        compiler_params=_cparams(1),
        name="ssm_discretise",
    )(lam_re, lam_im, log_dt, b_re, b_im)


def _ssm_readout(x_all, u_all, g_all, cmat_ref, d_ref, gw_ref, gb_ref):
    y = jnp.dot(x_all, cmat_ref[...], preferred_element_type=F32) + d_ref[...] * u_all
    y = jax.nn.gelu(y)
    y = y * _sigmoid(jnp.dot(y.astype(BF16), gw_ref[...], preferred_element_type=F32) + gb_ref[...])
    return y * _silu(g_all)


def _ssm_prompt_init(cst_ref):
    @pl.when(pl.program_id(1) == 0)
    def _():
        cst_ref[...] = jnp.zeros_like(cst_ref)


def _ssm_prompt_finish(sre_ref, sim_ref, cst_ref):
    @pl.when(pl.program_id(1) == pl.num_programs(1) - 1)
    def _():
        sre_ref[...] = cst_ref[0:1, 0:N_STATE]
        sim_ref[...] = cst_ref[0:1, N_STATE:2 * N_STATE]


def _ssm_prompt_main(r_steps, nr, zb_ref, bmat_ref, cmat_ref, ab_ref, apow_ref, d_ref, gw_ref, gb_ref,
                     yb_ref, u_ref, g_ref, xs_ref, xb_ref, cin_ref, cst_ref):
    n, c = N_STATE, D_SSM

    for s in range(r_steps):
        u_ref[s] = zb_ref[:, s * 2 * c:s * 2 * c + c]
        g_ref[s] = zb_ref[:, s * 2 * c + c:(s + 1) * 2 * c]
    u_all = u_ref[...].reshape(r_steps * nr, c)
    xs_ref[...] = jnp.dot(u_all.astype(BF16), bmat_ref[...],
                          preferred_element_type=F32).reshape(r_steps, nr, 2 * n)

    lane_block = 2 * LANES
    for lo in range(0, n, lane_block):
        hi = lo + lane_block
        ar, ai = ab_ref[0:1, lo:hi], ab_ref[1:2, lo:hi]
        xr = jnp.zeros((nr, lane_block), F32)
        xi = jnp.zeros((nr, lane_block), F32)
        for s in range(r_steps):
            nxr = ar * xr - ai * xi + xs_ref[s, :, lo:hi]
            nxi = ar * xi + ai * xr + xs_ref[s, :, n + lo:n + hi]
            xr, xi = nxr, nxi
            xs_ref[s, :, lo:hi] = xr
            xs_ref[s, :, n + lo:n + hi] = xi

    pr_end, pi_end = apow_ref[r_steps - 1:r_steps, 0:n], apow_ref[r_steps - 1:r_steps, n:2 * n]
    cr, ci = cst_ref[0:1, 0:n], cst_ref[0:1, n:2 * n]
    for i in range(nr):
        cin_ref[i:i + 1, 0:n] = cr
        cin_ref[i:i + 1, n:2 * n] = ci
        er = xs_ref[r_steps - 1, i:i + 1, 0:n]
        ei = xs_ref[r_steps - 1, i:i + 1, n:2 * n]
        cr, ci = pr_end * cr - pi_end * ci + er, pr_end * ci + pi_end * cr + ei
    cst_ref[0:1, 0:n] = cr
    cst_ref[0:1, n:2 * n] = ci

    for s in range(r_steps):
        pr, pi = apow_ref[s:s + 1, 0:n], apow_ref[s:s + 1, n:2 * n]
        cinr, cini = cin_ref[:, 0:n], cin_ref[:, n:2 * n]
        xb_ref[s, :, 0:n] = (xs_ref[s, :, 0:n] + pr * cinr - pi * cini).astype(BF16)
        xb_ref[s, :, n:2 * n] = (xs_ref[s, :, n:2 * n] + pr * cini + pi * cinr).astype(BF16)

    yb = _ssm_readout(xb_ref[...].reshape(r_steps * nr, 2 * n), u_all,
                      g_ref[...].reshape(r_steps * nr, c), cmat_ref, d_ref, gw_ref, gb_ref)
    for s in range(r_steps):
        yb_ref[:, s * c:(s + 1) * c] = yb[s * nr:(s + 1) * nr]


def _ssm_sample_kernel(r_steps, zb_ref, s0r_ref, s0i_ref, bmat_ref, cmat_ref, ab_ref, d_ref,
                       gw_ref, gb_ref, yb_ref, sre_ref, sim_ref, u_ref, g_ref, xs_ref, xb_ref):
    n, c = N_STATE, D_SSM
    nr = zb_ref.shape[0]
    for s in range(r_steps):
        u_ref[s] = zb_ref[:, s * 2 * c:s * 2 * c + c]
        g_ref[s] = zb_ref[:, s * 2 * c + c:(s + 1) * 2 * c]
    u_all = u_ref[...].reshape(r_steps * nr, c)
    xs_ref[...] = jnp.dot(u_all.astype(BF16), bmat_ref[...],
                          preferred_element_type=F32).reshape(r_steps, nr, 2 * n)
    for lo in range(0, n, LANES):
        hi = lo + LANES
        ar, ai = ab_ref[0:1, lo:hi], ab_ref[1:2, lo:hi]
        xr, xi = s0r_ref[:, lo:hi], s0i_ref[:, lo:hi]
        for s in range(r_steps):
            nxr = ar * xr - ai * xi + xs_ref[s, :, lo:hi]
            nxi = ar * xi + ai * xr + xs_ref[s, :, n + lo:n + hi]
            xr, xi = nxr, nxi
            xb_ref[s, :, lo:hi] = xr.astype(BF16)
            xb_ref[s, :, n + lo:n + hi] = xi.astype(BF16)
        sre_ref[:, lo:hi] = xr
        sim_ref[:, lo:hi] = xi
    yb = _ssm_readout(xb_ref[...].reshape(r_steps * nr, 2 * n), u_all,
                      g_ref[...].reshape(r_steps * nr, c), cmat_ref, d_ref, gw_ref, gb_ref)
    for s in range(r_steps):
        yb_ref[:, s * c:(s + 1) * c] = yb[s * nr:(s + 1) * nr]


def _ssm_sample(zb, nb, seq_len, s0_re, s0_im, bmat, cmat, ab, d, gw, gb):
    r, c, n = seq_len, D_SSM, N_STATE
    zbv = zb.reshape(nb, r * 2 * c)
    yb, s_re, s_im = pl.pallas_call(
        functools.partial(_ssm_sample_kernel, r),
        grid=(1,),
        in_specs=[_full((nb, r * 2 * c)), _full((nb, n)), _full((nb, n)),
                  _full((c, 2 * n)), _full((2 * n, c)), _full((2, n)),
                  _full((1, c)), _full((c, c)), _full((1, c))],
        out_specs=[_full((nb, r * c)), _full((nb, n)), _full((nb, n))],
        out_shape=[jax.ShapeDtypeStruct((nb, r * c), F32),
                   jax.ShapeDtypeStruct((nb, n), F32),
                   jax.ShapeDtypeStruct((nb, n), F32)],
        scratch_shapes=[pltpu.VMEM((r, nb, c), F32), pltpu.VMEM((r, nb, c), F32),
                        pltpu.VMEM((r, nb, 2 * n), F32), pltpu.VMEM((r, nb, 2 * n), BF16)],
        compiler_params=_cparams(1),
        name="ssm_sample",
    )(zbv, s0_re, s0_im, bmat, cmat, ab, d, gw, gb)
    return yb.reshape(nb * seq_len, c), s_re, s_im


def _prompt_front_kernel(n_blk, nr, x_ref, g_ref, w_ref, qg_ref, kg_ref, e_ref,
                         cw_ref, cb_ref, wg_ref, bg_ref, lam_ref,
                         bmat_ref, cmat_ref, ab_ref, apow_ref, d_ref, gw_ref, gb_ref,
                         q_ref, k_ref, v_ref, gc_ref, kf_ref, vf_ref,
                         ya_ref, conv_ref, hl_ref, yb_ref, sre_ref, sim_ref,
                         hn_ref, za_ref, zb_ref,
                         xc_ref, a_ref, b_ref, cina_ref, cx_ref, ch_ref,
                         u_ref, g2_ref, xs_ref, xb_ref, cinb_ref, cst_ref):
    _lru_prompt_init(cx_ref, ch_ref)
    _ssm_prompt_init(cst_ref)
    _inproj_kernel(n_blk, x_ref, g_ref, w_ref, qg_ref, kg_ref, e_ref,
                   za_ref, zb_ref, q_ref, k_ref, v_ref, gc_ref, kf_ref, vf_ref, hn_ref)
    _lru_prompt_main(n_blk, nr, za_ref, cw_ref, cb_ref, wg_ref, bg_ref, lam_ref, ya_ref,
                     xc_ref, a_ref, b_ref, cina_ref, cx_ref, ch_ref)
    _ssm_prompt_main(n_blk, nr, zb_ref, bmat_ref, cmat_ref, ab_ref, apow_ref, d_ref, gw_ref, gb_ref,
                     yb_ref, u_ref, g2_ref, xs_ref, xb_ref, cinb_ref, cst_ref)
    _lru_prompt_finish(n_blk, nr, za_ref, conv_ref, hl_ref, ch_ref)
    _ssm_prompt_finish(sre_ref, sim_ref, cst_ref)


def _prompt_front(x3, keep_rows, layer, kv_stack, norm_g, w_in_bf, qg, kg, emat, cw, cb, wg, bg, lam,
                  bmat, cmat, ab, apow, d, gw, gb):
    r, nr, n = CHUNK, CHUNK_ROWS, N_STATE
    nb, nrows, _ = x3.shape
    assert nrows % nr == 0 and keep_rows % nr == 0
    skip = (nrows - keep_rows) // nr
    kv_spec = pl.BlockSpec((None, None, nr, r * D_ATT),
                           lambda b, t: (layer, b, jnp.maximum(t - skip, 0), 0))
    n_in = 18
    stack_in = list(kv_stack)

    def body(*refs):
        _prompt_front_kernel(r, nr, *refs[:n_in], *refs[n_in + len(stack_in):])

    def per_seq(rows, width):
        return pl.BlockSpec((None, rows, width), lambda b, t: (b, 0, 0))

    def out(c, dtype=F32):
        return jax.ShapeDtypeStruct((nb, nrows, r * c), dtype)

    def seq_out(rows, width):
        return jax.ShapeDtypeStruct((nb, rows, width), F32)

    kv_shape = jax.ShapeDtypeStruct((DEPTH, nb, keep_rows, r * D_ATT), F32)
    att = _tile3(nr, r * D_ATT)
    res = pl.pallas_call(
        body,
        grid=(nb, nrows // nr),
        in_specs=[_tile3(nr, r * D_MODEL), _full((1, D_MODEL)), _full((D_MODEL, D_IN)),
                  _full((1, D_ATT)), _full((1, D_ATT)), _full((D_ATT, D_ATT)),
                  _full((CONV_W, D_LRU)), _full((1, D_LRU)), _full((D_LRU, 2 * D_LRU)),
                  _full((1, 2 * D_LRU)), _full((1, D_LRU)),
                  _full((D_SSM, 2 * n)), _full((2 * n, D_SSM)), _full((2, n)), _full((r, 2 * n)),
                  _full((1, D_SSM)), _full((D_SSM, D_SSM)), _full((1, D_SSM))]
                 + [pl.BlockSpec(memory_space=pl.ANY)] * len(stack_in),
        input_output_aliases={n_in + i: 4 + i for i in range(len(stack_in))},
        out_specs=[att, att, att, att, kv_spec, kv_spec,
                   _tile3(nr, r * D_LRU), per_seq(CONV_W - 1, D_LRU), per_seq(1, D_LRU),
                   _tile3(nr, r * D_SSM), per_seq(1, n), per_seq(1, n)],
        out_shape=[out(D_ATT, BF16), out(D_ATT, BF16), out(D_ATT, BF16), out(D_ATT), kv_shape, kv_shape,
                   out(D_LRU), seq_out(CONV_W - 1, D_LRU), seq_out(1, D_LRU),
                   out(D_SSM), seq_out(1, n), seq_out(1, n)],
        scratch_shapes=[pltpu.VMEM((r * nr, D_MODEL), BF16),
                        pltpu.VMEM((nr, r * 2 * D_LRU), F32), pltpu.VMEM((nr, r * 2 * D_SSM), F32),
                        pltpu.VMEM((r, nr, D_LRU), F32), pltpu.VMEM((r, nr, D_LRU), F32),
                        pltpu.VMEM((r, nr, D_LRU), F32), pltpu.VMEM((nr, D_LRU), F32),
                        pltpu.VMEM((8, D_LRU), F32), pltpu.VMEM((8, D_LRU), F32),
                        pltpu.VMEM((r, nr, D_SSM), F32), pltpu.VMEM((r, nr, D_SSM), F32),
                        pltpu.VMEM((r, nr, 2 * n), F32), pltpu.VMEM((r, nr, 2 * n), BF16),
                        pltpu.VMEM((nr, 2 * n), F32), pltpu.VMEM((8, 2 * n), F32)],
        compiler_params=_cparams(2),
        name="prompt_front",
    )(x3, norm_g, w_in_bf, qg, kg, emat, cw, cb, wg, bg, lam, bmat, cmat, ab, apow, d, gw, gb, *stack_in)
    q, k, v, gc, k_stack, v_stack, ya, conv_new, h_last, yb, s_re, s_im = res
    return (q, k, v, gc, (k_stack, v_stack), ya, conv_new, h_last.reshape(nb, D_LRU),
            yb, s_re.reshape(nb, n), s_im.reshape(nb, n))


def _attend(q, kk, vv, bias_ref, first):
    blk = q.shape[0]
    lane = lax.broadcasted_iota(jnp.int32, (1, LANES), 1)
    outs, lses = [], []
    for pair in range(N_HEADS // 2):
        sl = slice(pair * LANES, (pair + 1) * LANES)
        qp, kp, vp = q[:, sl], kk[:, sl], vv[:, sl]
        acc = jnp.zeros((blk, LANES), F32)
        lse = jnp.zeros((blk, LANES), F32)
        for hh in range(2):
            head_lanes = (lane >= HEAD_DIM) if hh else (lane < HEAD_DIM)
            qh = jnp.where(head_lanes, qp, jnp.zeros_like(qp))
            s = lax.dot_general(qh, kp, (((1,), (1,)), ((), ())), preferred_element_type=F32)
            s = s + bias_ref[2 * pair + hh + N_HEADS * first]
            m = jnp.max(s, axis=-1, keepdims=True)
            e = jnp.exp2(s - m)
            den = jnp.sum(e, axis=-1, keepdims=True)
            vh = jnp.where(head_lanes, vp, jnp.zeros_like(vp))
            pv = jnp.dot(e.astype(BF16), vh, preferred_element_type=F32)
            acc = acc + pv * (1.0 / den)
            lse = jnp.where(head_lanes, m + jnp.log2(den), lse)
        outs.append(acc)
        lses.append(lse)
    return outs, lses


SUB_ROWS = ATT_BLOCK // (CHUNK // 4)
D1_ROWS = 2 * ATT_BLOCK // CHUNK
HALF_ROWS = D1_ROWS // 2
D16_UNROLL = 4
SEQ_PER_STEP = CHUNK // D16_UNROLL + 3 * (ATT_BLOCK // SUB_ROWS)


def _put_pairs(dst, lead, rows, vals, src_rows=None):
    for pair in range(N_HEADS // 2):
        v = vals[pair] if src_rows is None else vals[pair][src_rows]
        dst[lead, rows, pair * LANES:(pair + 1) * LANES] = v


def _cache_copy(hbm, buf, sem, layer, seq, slot, which):
    return pltpu.make_async_copy(hbm.at[layer, seq], buf.at[slot], sem.at[which, slot])


def _attn_kernel(layer, n_seq, t_new,
                 q_ref, k_ref, v_ref, b16_ref, b4_ref, b1_ref,
                 qs_ref, kn_ref, vn_ref, kt_hbm, vt_hbm, sbias_ref,
                 o_ref, os_ref,
                 qst, kst, vst, o16, l16, o4, l4, o1, l1, qq, kk, vv, qf, kf, vf, qq1, kk1, vv1,
                 kbuf, vbuf, sem):
    blk, c = ATT_BLOCK, D_ATT
    jb = pl.program_id(1)
    step = pl.program_id(0) * pl.num_programs(1) + jb
    n_steps = pl.num_programs(0) * pl.num_programs(1)
    first = jnp.where(jb == 0, 1, 0)

    @pl.when(step == 0)
    def _():
        _cache_copy(kt_hbm, kbuf, sem, layer, 0, 0, 0).start()
        _cache_copy(vt_hbm, vbuf, sem, layer, 0, 0, 1).start()

    @pl.when(jb == 0)
    def _():
        for s in range(CHUNK):
            kst[s, 0:blk, :] = jnp.zeros((blk, c), BF16)
            vst[s, 0:blk, :] = jnp.zeros((blk, c), BF16)

    for s in range(CHUNK):
        sl = slice(s * c, (s + 1) * c)
        qst[s] = q_ref[:, sl]
        kst[s, blk:2 * blk, :] = k_ref[:, sl]
        vst[s, blk:2 * blk, :] = v_ref[:, sl]

    rows_qh = N_HEADS * t_new
    row_head = lax.broadcasted_iota(jnp.int32, (rows_qh, c), 0) // t_new
    lane_head = lax.broadcasted_iota(jnp.int32, (rows_qh, c), 1) // HEAD_DIM
    own = row_head == lane_head
    nt = (((1,), (1,)), ((), ()))

    def sample_seq(local):
        g = step * SEQ_PER_STEP + local
        slot = lax.rem(g, 2)
        nxt = jnp.minimum(g + 1, n_seq - 1)
        _cache_copy(kt_hbm, kbuf, sem, layer, nxt, 1 - slot, 0).start()
        _cache_copy(vt_hbm, vbuf, sem, layer, nxt, 1 - slot, 1).start()
        _cache_copy(kt_hbm, kbuf, sem, layer, g, slot, 0).wait()
        _cache_copy(vt_hbm, vbuf, sem, layer, g, slot, 1).wait()
        kt = kbuf[slot].reshape(c, PAST_LEN).astype(BF16)
        vt = vbuf[slot].reshape(c, PAST_LEN).astype(BF16)
        pad = jnp.zeros((LANES - t_new, c), F32)
        kn = jnp.concatenate([kn_ref[local], pad], axis=0).astype(BF16)
        vn = jnp.concatenate([vn_ref[local], pad], axis=0).astype(BF16)
        q8 = qs_ref[local].astype(F32)
        qbd = jnp.where(own, jnp.concatenate([q8] * N_HEADS, axis=0), 0.0).astype(BF16)
        s = jnp.concatenate([jnp.dot(qbd, kt, preferred_element_type=F32),
                             lax.dot_general(qbd, kn, nt, preferred_element_type=F32)], axis=1)
        es, dens, lses = [], [], []
        for p, (window, _) in enumerate(DILATED):
            lo = PAST_LEN - window
            sp = s[:, lo:] + sbias_ref[p, :, lo:]
            m = jnp.max(sp, axis=-1, keepdims=True)
            e = jnp.exp2(sp - m)
            den = jnp.sum(e, axis=-1, keepdims=True)
            es.append(e)
            dens.append(den)
            lses.append(m + jnp.log2(den))
        top = jnp.maximum(jnp.maximum(lses[0], lses[1]), lses[2])
        ws = [jnp.exp2(l - top) for l in lses]
        wsum = ws[0] + ws[1] + ws[2]
        cs = [w / (den * wsum) for w, den in zip(ws, dens)]
        cut1, cut2 = PAST_LEN - DILATED[1][0], PAST_LEN - DILATED[0][0]
        far = cs[2] * es[2][:, :cut1]
        mid = cs[2] * es[2][:, cut1:cut2] + cs[1] * es[1][:, :cut2 - cut1]
        near = cs[2] * es[2][:, cut2:] + cs[1] * es[1][:, cut2 - cut1:] + cs[0] * es[0]
        pc = jnp.concatenate([far, mid, near], axis=1).astype(BF16)
        o_all = (lax.dot_general(pc[:, :PAST_LEN], vt, nt, preferred_element_type=F32)
                 + jnp.dot(pc[:, PAST_LEN:], vn, preferred_element_type=F32))
        o_all = jnp.where(own, o_all, 0.0)
        out = o_all[0:t_new]
        for h in range(1, N_HEADS):
            out = out + o_all[h * t_new:(h + 1) * t_new]
        os_ref[local] = out

    def d16_body(i, carry):
        for u in range(D16_UNROLL):
            s = i * D16_UNROLL + u
            outs, lses = _attend(qst.at[s], kst.at[s], vst.at[s], b16_ref, first)
            _put_pairs(o16, s, slice(None), outs)
            _put_pairs(l16, s, slice(None), lses)
        sample_seq(i)
        return carry

    n_d16 = CHUNK // D16_UNROLL
    lax.fori_loop(0, n_d16, d16_body, 0)

    def sub_body(j, carry):
        r0 = pl.multiple_of(j * SUB_ROWS, SUB_ROWS)
        first_j = first * jnp.where(j == 0, 1, 0)
        seq0 = n_d16 + 3 * j

        for r in range(4):
            for m in range(CHUNK // 4):
                s = r + 4 * m
                qq[m * SUB_ROWS:(m + 1) * SUB_ROWS, :] = qst[s, pl.ds(r0, SUB_ROWS), :]
                kk[2 * m * SUB_ROWS:2 * (m + 1) * SUB_ROWS, :] = kst[s, pl.ds(blk - SUB_ROWS + r0, 2 * SUB_ROWS), :]
                vv[2 * m * SUB_ROWS:2 * (m + 1) * SUB_ROWS, :] = vst[s, pl.ds(blk - SUB_ROWS + r0, 2 * SUB_ROWS), :]
            outs, lses = _attend(qq, kk, vv, b4_ref, first_j)
            for m in range(CHUNK // 4):
                _put_pairs(o4, r + 4 * m, slice(None), outs, slice(m * SUB_ROWS, (m + 1) * SUB_ROWS))
                _put_pairs(l4, r + 4 * m, slice(None), lses, slice(m * SUB_ROWS, (m + 1) * SUB_ROWS))
        sample_seq(seq0)

        for t2 in range(SUB_ROWS // D1_ROWS):
            rt = r0 + t2 * D1_ROWS
            for s in range(CHUNK):
                qp = qst[s, pl.ds(pl.multiple_of(rt, D1_ROWS), D1_ROWS), :].astype(F32)
                for part in range(2):
                    qf[part * blk + s * HALF_ROWS:part * blk + (s + 1) * HALF_ROWS, :] = (
                        qp[part * HALF_ROWS:(part + 1) * HALF_ROWS])
                for src, dst in ((kst, kf), (vst, vf)):
                    kp = src[s, pl.ds(pl.multiple_of(blk - D1_ROWS + rt, D1_ROWS), 2 * D1_ROWS), :].astype(F32)
                    for part in range(3):
                        dst[part * blk + s * HALF_ROWS:part * blk + (s + 1) * HALF_ROWS, :] = (
                            kp[(part + 1) * HALF_ROWS:(part + 2) * HALF_ROWS])
            qq1[...] = qf[...].astype(BF16)
            kk1[...] = kf[...].astype(BF16)
            vv1[...] = vf[...].astype(BF16)
            for part in range(2):
                sel = first_j if (t2 == 0 and part == 0) else 0
                outs, lses = _attend(qq1.at[pl.ds(part * blk, blk), :], kk1.at[pl.ds(part * blk, 2 * blk), :],
                                     vv1.at[pl.ds(part * blk, 2 * blk), :], b1_ref, sel)
                rows = slice(t2 * D1_ROWS + part * HALF_ROWS, t2 * D1_ROWS + (part + 1) * HALF_ROWS)
                for s in range(CHUNK):
                    _put_pairs(o1, s, rows, outs, slice(s * HALF_ROWS, (s + 1) * HALF_ROWS))
                    _put_pairs(l1, s, rows, lses, slice(s * HALF_ROWS, (s + 1) * HALF_ROWS))
        sample_seq(seq0 + 1)

        for s in range(CHUNK):
            for lo in range(0, c, LANES):
                ls = [l16[s, pl.ds(r0, SUB_ROWS), lo:lo + LANES], l4[s, :, lo:lo + LANES], l1[s, :, lo:lo + LANES]]
                os = [o16[s, pl.ds(r0, SUB_ROWS), lo:lo + LANES], o4[s, :, lo:lo + LANES], o1[s, :, lo:lo + LANES]]
                top = jnp.maximum(jnp.maximum(ls[0], ls[1]), ls[2])
                ws = [jnp.exp2(l - top) for l in ls]
                mix = (ws[0] * os[0] + ws[1] * os[1] + ws[2] * os[2]) / (ws[0] + ws[1] + ws[2])
                o_ref[pl.ds(r0, SUB_ROWS), s * c + lo:s * c + lo + LANES] = mix
        sample_seq(seq0 + 2)
        return carry

    lax.fori_loop(0, blk // SUB_ROWS, sub_body, 0)

    for s in range(CHUNK):
        for lo in range(0, c, LANES):
            kst[s, 0:blk, lo:lo + LANES] = kst[s, blk:2 * blk, lo:lo + LANES]
            vst[s, 0:blk, lo:lo + LANES] = vst[s, blk:2 * blk, lo:lo + LANES]

    @pl.when(step == n_steps - 1)
    def _():
        spare = 1 - (n_seq - 1) % 2
        _cache_copy(kt_hbm, kbuf, sem, layer, n_seq - 1, spare, 0).wait()
        _cache_copy(vt_hbm, vbuf, sem, layer, n_seq - 1, spare, 1).wait()


def _prompt_bias(q_idx, k_idx, dil):
    dist = q_idx[:, None] - k_idx[None, :]
    valid = (dist >= 0) & (dist <= ATT_J)
    bias = -(ALIBI_SLOPES * LOG2_E)[:, None, None] * (dist * dil).astype(np.float32)[None]
    table = np.where(valid[None], bias, NEG_INF)
    masked = np.where((k_idx >= 0)[None, None, :], table, NEG_INF)
    return jnp.asarray(np.concatenate([table, masked], axis=0), dtype=F32)


def _attention(q, k, v, qs, k_new, v_new, cache_kt, cache_vt, layer):
    blk, c = ATT_BLOCK, D_ATT
    nb, nrows, width = q.shape
    n_seq = cache_kt.shape[1]
    t_new = qs.size // (n_seq * c)
    n_steps = nb * (nrows // blk)
    assert nrows % blk == 0 and n_seq == n_steps * SEQ_PER_STEP and cache_kt.shape[-1] == PAST_LEN
    a = np.arange(blk)
    c2 = np.arange(2 * blk)
    b16 = _prompt_bias(a, c2 - blk, 16)
    k4 = 4 * (c2 % SUB_ROWS + SUB_ROWS * ((c2 // SUB_ROWS) % 2 - 1)) + c2 // (2 * SUB_ROWS)
    b4 = _prompt_bias(4 * (a % SUB_ROWS) + a // SUB_ROWS, k4, 4)
    order1 = CHUNK * (a % HALF_ROWS) + a // HALF_ROWS
    b1 = _prompt_bias(order1, np.concatenate([order1 - blk, order1]), 1)
    spec = _tile3(blk, width)
    bias_spec = _full((2 * N_HEADS, blk, 2 * blk))
    seq_spec = pl.BlockSpec((SEQ_PER_STEP, t_new, c), lambda b, t: (b * (nrows // blk) + t, 0, 0))
    hbm_spec = pl.BlockSpec(memory_space=pl.ANY)

    def st(rows, dtype):
        return pltpu.VMEM((CHUNK, rows, c), dtype)

    o, o_s = pl.pallas_call(
        functools.partial(_attn_kernel, layer, n_seq, t_new),
        grid=(nb, nrows // blk),
        in_specs=[spec, spec, spec, bias_spec, bias_spec, bias_spec,
                  seq_spec, seq_spec, seq_spec, hbm_spec, hbm_spec,
                  _full((len(DILATED), N_HEADS * t_new, PAST_LEN + LANES))],
        out_specs=[spec, seq_spec],
        out_shape=[jax.ShapeDtypeStruct((nb, nrows, width), F32),
                   jax.ShapeDtypeStruct((n_seq, t_new, c), F32)],
        scratch_shapes=[st(blk, BF16), st(2 * blk, BF16), st(2 * blk, BF16),
                        st(blk, F32), st(blk, F32),
                        st(SUB_ROWS, F32), st(SUB_ROWS, F32), st(SUB_ROWS, F32), st(SUB_ROWS, F32),
                        pltpu.VMEM((blk, c), BF16), pltpu.VMEM((2 * blk, c), BF16), pltpu.VMEM((2 * blk, c), BF16),
                        pltpu.VMEM((2 * blk, c), F32), pltpu.VMEM((3 * blk, c), F32), pltpu.VMEM((3 * blk, c), F32),
                        pltpu.VMEM((2 * blk, c), BF16), pltpu.VMEM((3 * blk, c), BF16), pltpu.VMEM((3 * blk, c), BF16),
                        pltpu.VMEM((2, N_HEADS, HEAD_DIM, PAST_LEN), F32),
                        pltpu.VMEM((2, N_HEADS, HEAD_DIM, PAST_LEN), F32),
                        pltpu.SemaphoreType.DMA((2, 2))],
        compiler_params=_cparams(2), name="attention",
    )(q, k, v, b16, b4, b1,
      qs.reshape(n_seq, t_new, c), k_new.reshape(n_seq, t_new, c), v_new.reshape(n_seq, t_new, c),
      cache_kt, cache_vt, _sample_bias(t_new))
    return o, o_s.reshape(n_seq * t_new, c)


def _sample_bias(t_new):
    col = np.arange(PAST_LEN + LANES)
    real = col < PAST_LEN + t_new
    t = np.arange(t_new)
    dist = PAST_LEN + t[:, None] - col[None, :]
    out = []
    for window, dil in DILATED:
        valid = real[None, :] & (dist >= 0) & (dist % dil == 0) & (dist <= window)
        bias = -(ALIBI_SLOPES * LOG2_E)[:, None, None] * dist.astype(np.float32)[None]
        bias = np.where(valid[None], bias, NEG_INF)
        out.append(bias.reshape(N_HEADS * t_new, PAST_LEN + LANES))
    return jnp.asarray(np.stack(out, axis=0), dtype=F32)


def _outproj_kernel(n_blk, x_ref, ya_ref, yb_ref, gc_ref, o_ref, og_ref, w_ref, out_ref, yn_ref):
    rows = x_ref.shape[0]
    a_hi, b_hi = D_LRU, D_LRU + D_SSM

    def norm_into(y, lo, hi, s):
        yn = y * lax.rsqrt(jnp.mean(y * y, axis=-1, keepdims=True) + EPS) * og_ref[:, lo:hi]
        yn_ref[s * rows:(s + 1) * rows, lo:hi] = yn.astype(BF16)

    for s in range(n_blk):
        att = slice(s * D_ATT, (s + 1) * D_ATT)
        yc = o_ref[:, att] * _silu(gc_ref[:, att])
        norm_into(ya_ref[:, s * D_LRU:(s + 1) * D_LRU], 0, a_hi, s)
        norm_into(yb_ref[:, s * D_SSM:(s + 1) * D_SSM], a_hi, b_hi, s)
        norm_into(yc, b_hi, D_MODEL, s)
    res = jnp.dot(yn_ref[...], w_ref[...], preferred_element_type=F32)
    for s in range(n_blk):
        sl = slice(s * D_MODEL, (s + 1) * D_MODEL)
        out_ref[:, sl] = x_ref[:, sl] + res[s * rows:(s + 1) * rows]


def _outproj(x3, n_blk, rows_t, ya, yb, gc, o, og, w_out_bf):
    nb, nrows, _ = x3.shape
    att_spec = _tile3(rows_t, n_blk * D_ATT)
    return pl.pallas_call(
        functools.partial(_outproj_kernel, n_blk),
        grid=(nb, nrows // rows_t),
        in_specs=[_tile3(rows_t, n_blk * D_MODEL), _tile3(rows_t, n_blk * D_LRU),
                  _tile3(rows_t, n_blk * D_SSM), att_spec, att_spec,
                  _full((1, D_MODEL)), _full((D_MODEL, D_MODEL))],
        out_specs=_tile3(rows_t, n_blk * D_MODEL),
        out_shape=jax.ShapeDtypeStruct(x3.shape, F32),
        scratch_shapes=[pltpu.VMEM((n_blk * rows_t, D_MODEL), BF16)],
        compiler_params=_cparams(2),
        name="outproj",
    )(x3, ya, yb, gc, o, og, w_out_bf)


def _block_diag(blocks):
    k, i, j = blocks.shape
    eye = jnp.eye(k, dtype=blocks.dtype)
    return jnp.einsum("kij,kl->kilj", blocks, eye).reshape(k * i, k * j)


def kernel(x_prompt, x_sample, state_conv, state_lru, state_ssm_re, state_ssm_im, cache_k, cache_v,
           norm_g, w_in, conv_w, conv_b, w_r, b_r, w_i, b_i, lru_lambda,
           ssm_lambda_re, ssm_lambda_im, ssm_log_dt, ssm_b_re, ssm_b_im, ssm_c_re, ssm_c_im,
           ssm_d, glu_w, glu_b, q_norm_g, k_norm_g, out_norm_g, w_out):
    bp, seq, _ = x_prompt.shape
    bs_, t_new, _ = x_sample.shape
    keep = min(PAST_LEN, seq)
    assert seq % (CHUNK * PROJ_ROWS) == 0 and keep % (CHUNK * PROJ_ROWS) == 0
    n_s = bs_ * t_new
    rows_s = min(ROW_TILE, n_s)
    xp = x_prompt.reshape(bp, seq // CHUNK, CHUNK * D_MODEL)
    xs = x_sample.reshape(1, n_s, D_MODEL)
    cache_kt = jnp.transpose(cache_k, (0, 1, 3, 4, 2))
    cache_vt = jnp.transpose(cache_v, (0, 1, 3, 4, 2))

    head_of = jnp.arange(D_ATT) // HEAD_DIM
    emat = (head_of[:, None] == head_of[None, :]).astype(BF16)

    def chan_major(b):
        return jnp.transpose(b, (0, 3, 1, 2)).reshape(DEPTH, SSM_GROUP, N_STATE)

    def vec(a):
        return a.reshape(DEPTH, 1, N_STATE)

    ab_all, bbr_all, bbi_all, apow_all = _ssm_discretise(
        vec(ssm_lambda_re), vec(ssm_lambda_im), vec(ssm_log_dt), chan_major(ssm_b_re), chan_major(ssm_b_im))

    sp_list, ss_list = [], []
    kv_stack = tuple(jnp.zeros((DEPTH, bp, keep // CHUNK, CHUNK * D_ATT), F32) for _ in range(2))
    for l in range(DEPTH):
        w_in_bf = w_in[l].astype(BF16)
        w_out_bf = w_out[l].astype(BF16)
        ng = norm_g[l].reshape(1, D_MODEL)
        qg = jnp.tile(q_norm_g[l], N_HEADS).reshape(1, D_ATT)
        kg = jnp.tile(k_norm_g[l], N_HEADS).reshape(1, D_ATT)
        og = out_norm_g[l].reshape(1, D_MODEL)
        cw, cb = conv_w[l], conv_b[l].reshape(1, D_LRU)
        wg = jnp.concatenate([_block_diag(w_r[l]), _block_diag(w_i[l])], axis=1).astype(BF16)
        bg = jnp.concatenate([b_r[l], b_i[l]]).reshape(1, 2 * D_LRU)
        lam = lru_lambda[l].reshape(1, D_LRU)

        def b_matrix(bb):
            blocks = jnp.transpose(bb.reshape(SSM_GROUP, SSM_GROUPS, SSM_STATE), (1, 0, 2))
            return _block_diag(blocks)
        bmat = jnp.concatenate([b_matrix(bbr_all[l]), b_matrix(bbi_all[l])], axis=1).astype(BF16)
        cmat = jnp.concatenate([_block_diag(jnp.transpose(ssm_c_re[l], (0, 2, 1))),
                                -_block_diag(jnp.transpose(ssm_c_im[l], (0, 2, 1)))], axis=0).astype(BF16)
        d = ssm_d[l].reshape(1, D_SSM)
        gw = glu_w[l].astype(BF16)
        gb = glu_b[l].reshape(1, D_SSM)

        q, k, v, gc, kv_stack, ya, conv_new, h_last, yb, s_re, s_im = _prompt_front(
            xp, keep // CHUNK, l, kv_stack, ng, w_in_bf, qg, kg, emat, cw, cb, wg, bg, lam,
            bmat, cmat, ab_all[l], apow_all[l], d, gw, gb)
        sp_list.append((conv_new, h_last,
                        s_re.reshape(bp, SSM_GROUPS, SSM_STATE), s_im.reshape(bp, SSM_GROUPS, SSM_STATE)))

        zas, zbs, qs, _, _, gcs, ks_rows, vs_rows = _inproj(xs, 1, rows_s, n_s, ng, w_in_bf, qg, kg, emat)
        yas, conv_new, h_last = _lru_sample(zas, bs_, t_new, state_conv[l], state_lru[l], cw, cb, wg, bg, lam)
        ybs, s_re, s_im = _ssm_sample(zbs, bs_, t_new, state_ssm_re[l].reshape(bs_, N_STATE),
                                      state_ssm_im[l].reshape(bs_, N_STATE), bmat, cmat, ab_all[l], d, gw, gb)
        ss_list.append((conv_new, h_last,
                        s_re.reshape(bs_, SSM_GROUPS, SSM_STATE), s_im.reshape(bs_, SSM_GROUPS, SSM_STATE),
                        ks_rows.reshape(bs_, t_new, N_HEADS, HEAD_DIM), vs_rows.reshape(bs_, t_new, N_HEADS, HEAD_DIM)))

        o, o_s = _attention(q, k, v, qs, ks_rows, vs_rows, cache_kt, cache_vt, l)
        xp = _outproj(xp, CHUNK, PROJ_ROWS, ya, yb, gc, o, og, w_out_bf)
        xs = _outproj(xs, 1, rows_s, yas.reshape(1, n_s, D_LRU), ybs.reshape(1, n_s, D_SSM), gcs,
                      o_s.reshape(1, n_s, D_ATT), og, w_out_bf)

    def stack(lst, i):
        return jnp.stack([s[i] for s in lst], axis=0)

    return (xp.reshape(bp, seq, D_MODEL), xs.reshape(bs_, t_new, D_MODEL),
            stack(sp_list, 0), stack(sp_list, 1), stack(sp_list, 2), stack(sp_list, 3),
            kv_stack[0].reshape(DEPTH, bp, keep, N_HEADS, HEAD_DIM),
            kv_stack[1].reshape(DEPTH, bp, keep, N_HEADS, HEAD_DIM),
            stack(ss_list, 0), stack(ss_list, 1), stack(ss_list, 2), stack(ss_list, 3),
            stack(ss_list, 4), stack(ss_list, 5))
```

```python
import functools

import numpy as np
import jax
import jax.numpy as jnp
from jax import lax
from jax.experimental import pallas as pl
from jax.experimental.pallas import tpu as pltpu

F32 = jnp.float32
BF16 = jnp.bfloat16

D_MODEL = 1024
DEPTH = 4
PAST_LEN = 2048
D_LRU = 384
CONV_W = 4
RG_C = 8.0
D_SSM = 256
SSM_GROUP = 16
SSM_GROUPS = 16
SSM_STATE = 64
N_STATE = SSM_GROUPS * SSM_STATE
D_ATT = 384
HEAD_DIM = 64
N_HEADS = 6
DILATED = ((128, 1), (512, 4), (2048, 16))
ATT_J = 128
D_IN = 2 * D_LRU + 2 * D_SSM + 4 * D_ATT
EPS = 1e-6
NEG_INF = -1e30
LOG2_E = float(np.log2(np.e))
ALIBI_SLOPES = np.exp2(-8.0 * np.arange(1, N_HEADS + 1, dtype=np.float32) / N_HEADS).astype(np.float32)

C_ZA = (0, 2 * D_LRU)
C_ZB = (2 * D_LRU, 2 * D_LRU + 2 * D_SSM)
C_Q = (C_ZB[1], C_ZB[1] + D_ATT)
C_K = (C_Q[1], C_Q[1] + D_ATT)
C_V = (C_K[1], C_K[1] + D_ATT)
C_GC = (C_V[1], C_V[1] + D_ATT)

V7X_VMEM_LIMIT = 56 * 1024 * 1024
LANES = 128
CHUNK = 16
CHUNK_ROWS = 32
PROJ_ROWS = 64
ROW_TILE = 512
ATT_BLOCK = ATT_J


def _cparams(n_axes):
    return pltpu.CompilerParams(dimension_semantics=("arbitrary",) * n_axes,
                                vmem_limit_bytes=V7X_VMEM_LIMIT)


def _sigmoid(x):
    return 1.0 / (1.0 + jnp.exp(-x))


def _silu(x):
    return x * _sigmoid(x)


def _softplus(x):
    return jnp.maximum(x, 0.0) + jnp.log1p(jnp.exp(-jnp.abs(x)))


def _full(shape):
    nd = len(shape)
    return pl.BlockSpec(shape, lambda *_: (0,) * nd)


def _tile3(rows, width):
    return pl.BlockSpec((None, rows, width), lambda b, t: (b, t, 0))


def _inproj_kernel(n_blk, x_ref, g_ref, w_ref, qg_ref, kg_ref, e_ref,
                   za_ref, zb_ref, q_ref, k_ref, v_ref, gc_ref, kf_ref, vf_ref, hn_ref):
    rows = x_ref.shape[0]
    for s in range(n_blk):
        x = x_ref[:, s * D_MODEL:(s + 1) * D_MODEL]
        ms = jnp.mean(x * x, axis=-1, keepdims=True)
        hn_ref[s * rows:(s + 1) * rows, :] = (x * lax.rsqrt(ms + EPS) * g_ref[...]).astype(BF16)
    hn = hn_ref[...]

    def proj(cols):
        return jnp.dot(hn, w_ref[:, cols[0]:cols[1]], preferred_element_type=F32)

    def head_norm(z, gain):
        ss = jnp.dot((z * z).astype(BF16), e_ref[...], preferred_element_type=F32)
        return z * lax.rsqrt(ss * (1.0 / HEAD_DIM) + EPS) * gain

    def put(ref, z):
        c = z.shape[1]
        for s in range(n_blk):
            ref[:, s * c:(s + 1) * c] = z[s * rows:(s + 1) * rows, :].astype(ref.dtype)

    put(za_ref, proj(C_ZA))
    put(zb_ref, proj(C_ZB))
    q = head_norm(proj(C_Q), qg_ref[...]) * (HEAD_DIM ** -0.5 * LOG2_E)
    k = head_norm(proj(C_K), kg_ref[...])
    v = proj(C_V)
    put(q_ref, q)
    put(k_ref, k)
    put(v_ref, v)
    put(kf_ref, k)
    put(vf_ref, v)
    put(gc_ref, proj(C_GC))


def _inproj(x3, n_blk, rows_t, keep_rows, norm_g, w_in_bf, qg, kg, emat):
    nb, nrows, _ = x3.shape
    assert nrows % rows_t == 0 and keep_rows % rows_t == 0
    skip = (nrows - keep_rows) // rows_t
    kv_spec = pl.BlockSpec((None, rows_t, n_blk * D_ATT), lambda b, t: (b, jnp.maximum(t - skip, 0), 0))

    def out(c, dtype=F32):
        return jax.ShapeDtypeStruct((nb, nrows, n_blk * c), dtype)

    kv_shape = jax.ShapeDtypeStruct((nb, keep_rows, n_blk * D_ATT), F32)
    return pl.pallas_call(
        functools.partial(_inproj_kernel, n_blk),
        grid=(nb, nrows // rows_t),
        in_specs=[_tile3(rows_t, n_blk * D_MODEL), _full((1, D_MODEL)), _full((D_MODEL, D_IN)),
                  _full((1, D_ATT)), _full((1, D_ATT)), _full((D_ATT, D_ATT))],
        out_specs=[_tile3(rows_t, n_blk * 2 * D_LRU), _tile3(rows_t, n_blk * 2 * D_SSM),
                   _tile3(rows_t, n_blk * D_ATT), _tile3(rows_t, n_blk * D_ATT),
                   _tile3(rows_t, n_blk * D_ATT), _tile3(rows_t, n_blk * D_ATT), kv_spec, kv_spec],
        out_shape=[out(2 * D_LRU), out(2 * D_SSM), out(D_ATT, BF16), out(D_ATT, BF16),
                   out(D_ATT, BF16), out(D_ATT), kv_shape, kv_shape],
        scratch_shapes=[pltpu.VMEM((n_blk * rows_t, D_MODEL), BF16)],
        compiler_params=_cparams(2),
        name="inproj",
    )(x3, norm_g, w_in_bf, qg, kg, emat)


def _lru_conv(xa, prev, cw_ref, cb_ref, xc_ref, r_steps):
    for s in range(r_steps):
        acc = cb_ref[...] + cw_ref[3:4, :] * xa(s)
        for back in (1, 2, 3):
            src = xa(s - back) if s - back >= 0 else prev[back - s]
            acc = acc + cw_ref[3 - back:4 - back, :] * src
        xc_ref[s] = acc


def _lru_gates(xc, wg_ref, bg_ref, lam_ref):
    g = jnp.dot(xc.astype(BF16), wg_ref[...], preferred_element_type=F32) + bg_ref[...]
    r = _sigmoid(g[:, :D_LRU])
    i = _sigmoid(g[:, D_LRU:])
    log_a = -RG_C * r * _softplus(-lam_ref[...])
    a = jnp.exp(log_a)
    mult = jnp.sqrt(-jnp.tanh(log_a) * (a * a + 1.0))
    return a, mult, i


def _lru_prompt_init(cx_ref, ch_ref):
    @pl.when(pl.program_id(1) == 0)
    def _():
        cx_ref[...] = jnp.zeros_like(cx_ref)
        ch_ref[...] = jnp.zeros_like(ch_ref)


def _lru_prompt_finish(r_steps, nr, za_ref, conv_ref, hl_ref, ch_ref):
    c = D_LRU

    @pl.when(pl.program_id(1) == pl.num_programs(1) - 1)
    def _():
        hl_ref[...] = ch_ref[0:1, :]
        for j in range(CONV_W - 1):
            s = r_steps - (CONV_W - 1) + j
            conv_ref[j:j + 1, :] = za_ref[nr - 1:nr, s * 2 * c:s * 2 * c + c]


def _lru_prompt_main(r_steps, nr, za_ref, cw_ref, cb_ref, wg_ref, bg_ref, lam_ref, ya_ref,
                     xc_ref, a_ref, b_ref, cin_ref, cx_ref, ch_ref):
    c = D_LRU
    ti = pl.program_id(1)

    def xa(s):
        return za_ref[:, s * 2 * c:s * 2 * c + c]

    def ga(s):
        return za_ref[:, s * 2 * c + c:(s + 1) * 2 * c]

    row = lax.broadcasted_iota(jnp.int32, (nr, c), 0)
    prev = {j: jnp.where(row == 0, cx_ref[j - 1:j, :], pltpu.roll(xa(r_steps - j), 1, 0))
            for j in (1, 2, 3)}
    _lru_conv(xa, prev, cw_ref, cb_ref, xc_ref, r_steps)

    xc = xc_ref[...].reshape(r_steps * nr, c)
    a, mult, gate_i = _lru_gates(xc, wg_ref, bg_ref, lam_ref)
    flat_row = lax.broadcasted_iota(jnp.int32, (r_steps * nr, c), 0)
    mult = jnp.where(jnp.logical_and(flat_row == 0, ti == 0), 1.0, mult)
    a_ref[...] = a.reshape(r_steps, nr, c)
    b_ref[...] = (mult * gate_i * xc).reshape(r_steps, nr, c)

    h = jnp.zeros((nr, c), F32)
    p = jnp.ones((nr, c), F32)
    for s in range(r_steps):
        a_s = a_ref[s]
        h = a_s * h + b_ref[s]
        p = a_s * p
        b_ref[s] = h
        a_ref[s] = p

    carry = ch_ref[0:1, :]
    for i in range(nr):
        cin_ref[i:i + 1, :] = carry
        carry = a_ref[r_steps - 1, i:i + 1, :] * carry + b_ref[r_steps - 1, i:i + 1, :]
    ch_ref[0:1, :] = carry

    cin = cin_ref[...]
    for s in range(r_steps):
        h_s = b_ref[s] + a_ref[s] * cin
        ya_ref[:, s * c:(s + 1) * c] = h_s * _silu(ga(s))

    for j in (1, 2, 3):
        cx_ref[j - 1:j, :] = xa(r_steps - j)[nr - 1:nr, :]


def _lru_sample_kernel(r_steps, za_ref, sc_ref, h0_ref, cw_ref, cb_ref, wg_ref, bg_ref, lam_ref,
                       ya_ref, conv_ref, hl_ref, xc_ref, a_ref, b_ref):
    c = D_LRU
    nr = za_ref.shape[0]

    def xa(s):
        return za_ref[:, s * 2 * c:s * 2 * c + c]

    def ga(s):
        return za_ref[:, s * 2 * c + c:(s + 1) * 2 * c]

    prev = {j: sc_ref[:, (CONV_W - 1 - j) * c:(CONV_W - j) * c] for j in (1, 2, 3)}
    _lru_conv(xa, prev, cw_ref, cb_ref, xc_ref, r_steps)
    xc = xc_ref[...].reshape(r_steps * nr, c)
    a, mult, gate_i = _lru_gates(xc, wg_ref, bg_ref, lam_ref)
    a_ref[...] = a.reshape(r_steps, nr, c)
    b_ref[...] = (mult * gate_i * xc).reshape(r_steps, nr, c)
    h = h0_ref[...]
    for s in range(r_steps):
        h = a_ref[s] * h + b_ref[s]
        ya_ref[:, s * c:(s + 1) * c] = h * _silu(ga(s))
    hl_ref[...] = h
    for j in range(CONV_W - 1):
        conv_ref[:, j * c:(j + 1) * c] = xa(r_steps - (CONV_W - 1) + j)


def _lru_sample(za, nb, seq_len, state_conv, h0, cw, cb, wg, bg, lam):
    r, c = seq_len, D_LRU
    zav = za.reshape(nb, r * 2 * c)
    scv = state_conv.reshape(nb, (CONV_W - 1) * c)
    ya, conv_new, h_last = pl.pallas_call(
        functools.partial(_lru_sample_kernel, r),
        grid=(1,),
        in_specs=[_full((nb, r * 2 * c)), _full((nb, (CONV_W - 1) * c)), _full((nb, c)),
                  _full((CONV_W, c)), _full((1, c)), _full((c, 2 * c)), _full((1, 2 * c)),
                  _full((1, c))],
        out_specs=[_full((nb, r * c)), _full((nb, (CONV_W - 1) * c)), _full((nb, c))],
        out_shape=[jax.ShapeDtypeStruct((nb, r * c), F32),
                   jax.ShapeDtypeStruct((nb, (CONV_W - 1) * c), F32),
                   jax.ShapeDtypeStruct((nb, c), F32)],
        scratch_shapes=[pltpu.VMEM((r, nb, c), F32), pltpu.VMEM((r, nb, c), F32),
                        pltpu.VMEM((r, nb, c), F32)],
        compiler_params=_cparams(1),
        name="lru_sample",
    )(zav, scv, h0, cw, cb, wg, bg, lam)
    return ya.reshape(nb * seq_len, c), conv_new.reshape(nb, CONV_W - 1, c), h_last


def _ssm_disc_kernel(r_steps, lr_ref, li_ref, ldt_ref, br_ref, bi_ref,
                     ab_ref, bbr_ref, bbi_ref, apow_ref):
    lr, li = lr_ref[...], li_ref[...]
    dt = jnp.exp(ldt_ref[...])
    mag = jnp.exp(lr * dt)
    ang = li * dt
    ab_re, ab_im = mag * jnp.cos(ang), mag * jnp.sin(ang)
    den = lr * lr + li * li
    xr, yi = ab_re - 1.0, ab_im
    coef_re = (xr * lr + yi * li) / den
    coef_im = (yi * lr - xr * li) / den
    br, bi = br_ref[...], bi_ref[...]
    bbr_ref[...] = coef_re * br - coef_im * bi
    bbi_ref[...] = coef_re * bi + coef_im * br
    ab_ref[0:1, :] = ab_re
    ab_ref[1:2, :] = ab_im
    pr, pi = ab_re, ab_im
    for s in range(r_steps):
        apow_ref[s:s + 1, 0:N_STATE] = pr
        apow_ref[s:s + 1, N_STATE:2 * N_STATE] = pi
        pr, pi = pr * ab_re - pi * ab_im, pr * ab_im + pi * ab_re


def _ssm_discretise(lam_re, lam_im, log_dt, b_re, b_im):
    def vec():
        return pl.BlockSpec((None, 1, N_STATE), lambda l: (l, 0, 0))

    def mat():
        return pl.BlockSpec((None, SSM_GROUP, N_STATE), lambda l: (l, 0, 0))

    return pl.pallas_call(
        functools.partial(_ssm_disc_kernel, CHUNK),
        grid=(DEPTH,),
        in_specs=[vec(), vec(), vec(), mat(), mat()],
        out_specs=[pl.BlockSpec((None, 2, N_STATE), lambda l: (l, 0, 0)), mat(), mat(),
                   pl.BlockSpec((None, CHUNK, 2 * N_STATE), lambda l: (l, 0, 0))],
        out_shape=[jax.ShapeDtypeStruct((DEPTH, 2, N_STATE), F32),
                   jax.ShapeDtypeStruct((DEPTH, SSM_GROUP, N_STATE), F32),
                   jax.ShapeDtypeStruct((DEPTH, SSM_GROUP, N_STATE), F32),
                   jax.ShapeDtypeStruct((DEPTH, CHUNK, 2 * N_STATE), F32)],
        compiler_params=_cparams(1),
        name="ssm_discretise",
    )(lam_re, lam_im, log_dt, b_re, b_im)


def _ssm_readout(x_all, u_all, g_all, cmat_ref, d_ref, gw_ref, gb_ref):
    y = jnp.dot(x_all, cmat_ref[...], preferred_element_type=F32) + d_ref[...] * u_all
    y = jax.nn.gelu(y)
    y = y * _sigmoid(jnp.dot(y.astype(BF16), gw_ref[...], preferred_element_type=F32) + gb_ref[...])
    return y * _silu(g_all)


def _ssm_prompt_init(cst_ref):
    @pl.when(pl.program_id(1) == 0)
    def _():
        cst_ref[...] = jnp.zeros_like(cst_ref)


def _ssm_prompt_finish(sre_ref, sim_ref, cst_ref):
    @pl.when(pl.program_id(1) == pl.num_programs(1) - 1)
    def _():
        sre_ref[...] = cst_ref[0:1, 0:N_STATE]
        sim_ref[...] = cst_ref[0:1, N_STATE:2 * N_STATE]


def _ssm_prompt_main(r_steps, nr, zb_ref, bmat_ref, cmat_ref, ab_ref, apow_ref, d_ref, gw_ref, gb_ref,
                     yb_ref, u_ref, g_ref, xs_ref, xb_ref, cin_ref, cst_ref):
    n, c = N_STATE, D_SSM

    for s in range(r_steps):
        u_ref[s] = zb_ref[:, s * 2 * c:s * 2 * c + c]
        g_ref[s] = zb_ref[:, s * 2 * c + c:(s + 1) * 2 * c]
    u_all = u_ref[...].reshape(r_steps * nr, c)
    xs_ref[...] = jnp.dot(u_all.astype(BF16), bmat_ref[...],
                          preferred_element_type=F32).reshape(r_steps, nr, 2 * n)

    lane_block = 2 * LANES
    for lo in range(0, n, lane_block):
        hi = lo + lane_block
        ar, ai = ab_ref[0:1, lo:hi], ab_ref[1:2, lo:hi]
        xr = jnp.zeros((nr, lane_block), F32)
        xi = jnp.zeros((nr, lane_block), F32)
        for s in range(r_steps):
            nxr = ar * xr - ai * xi + xs_ref[s, :, lo:hi]
            nxi = ar * xi + ai * xr + xs_ref[s, :, n + lo:n + hi]
            xr, xi = nxr, nxi
            xs_ref[s, :, lo:hi] = xr
            xs_ref[s, :, n + lo:n + hi] = xi

    pr_end, pi_end = apow_ref[r_steps - 1:r_steps, 0:n], apow_ref[r_steps - 1:r_steps, n:2 * n]
    cr, ci = cst_ref[0:1, 0:n], cst_ref[0:1, n:2 * n]
    for i in range(nr):
        cin_ref[i:i + 1, 0:n] = cr
        cin_ref[i:i + 1, n:2 * n] = ci
        er = xs_ref[r_steps - 1, i:i + 1, 0:n]
        ei = xs_ref[r_steps - 1, i:i + 1, n:2 * n]
        cr, ci = pr_end * cr - pi_end * ci + er, pr_end * ci + pi_end * cr + ei
    cst_ref[0:1, 0:n] = cr
    cst_ref[0:1, n:2 * n] = ci

    for s in range(r_steps):
        pr, pi = apow_ref[s:s + 1, 0:n], apow_ref[s:s + 1, n:2 * n]
        cinr, cini = cin_ref[:, 0:n], cin_ref[:, n:2 * n]
        xb_ref[s, :, 0:n] = (xs_ref[s, :, 0:n] + pr * cinr - pi * cini).astype(BF16)
        xb_ref[s, :, n:2 * n] = (xs_ref[s, :, n:2 * n] + pr * cini + pi * cinr).astype(BF16)

    yb = _ssm_readout(xb_ref[...].reshape(r_steps * nr, 2 * n), u_all,
                      g_ref[...].reshape(r_steps * nr, c), cmat_ref, d_ref, gw_ref, gb_ref)
    for s in range(r_steps):
        yb_ref[:, s * c:(s + 1) * c] = yb[s * nr:(s + 1) * nr]


def _ssm_sample_kernel(r_steps, zb_ref, s0r_ref, s0i_ref, bmat_ref, cmat_ref, ab_ref, d_ref,
                       gw_ref, gb_ref, yb_ref, sre_ref, sim_ref, u_ref, g_ref, xs_ref, xb_ref):
    n, c = N_STATE, D_SSM
    nr = zb_ref.shape[0]
    for s in range(r_steps):
        u_ref[s] = zb_ref[:, s * 2 * c:s * 2 * c + c]
        g_ref[s] = zb_ref[:, s * 2 * c + c:(s + 1) * 2 * c]
    u_all = u_ref[...].reshape(r_steps * nr, c)
    xs_ref[...] = jnp.dot(u_all.astype(BF16), bmat_ref[...],
                          preferred_element_type=F32).reshape(r_steps, nr, 2 * n)
    for lo in range(0, n, LANES):
        hi = lo + LANES
        ar, ai = ab_ref[0:1, lo:hi], ab_ref[1:2, lo:hi]
        xr, xi = s0r_ref[:, lo:hi], s0i_ref[:, lo:hi]
        for s in range(r_steps):
            nxr = ar * xr - ai * xi + xs_ref[s, :, lo:hi]
            nxi = ar * xi + ai * xr + xs_ref[s, :, n + lo:n + hi]
            xr, xi = nxr, nxi
            xb_ref[s, :, lo:hi] = xr.astype(BF16)
            xb_ref[s, :, n + lo:n + hi] = xi.astype(BF16)
        sre_ref[:, lo:hi] = xr
        sim_ref[:, lo:hi] = xi
    yb = _ssm_readout(xb_ref[...].reshape(r_steps * nr, 2 * n), u_all,
                      g_ref[...].reshape(r_steps * nr, c), cmat_ref, d_ref, gw_ref, gb_ref)
    for s in range(r_steps):
        yb_ref[:, s * c:(s + 1) * c] = yb[s * nr:(s + 1) * nr]


def _ssm_sample(zb, nb, seq_len, s0_re, s0_im, bmat, cmat, ab, d, gw, gb):
    r, c, n = seq_len, D_SSM, N_STATE
    zbv = zb.reshape(nb, r * 2 * c)
    yb, s_re, s_im = pl.pallas_call(
        functools.partial(_ssm_sample_kernel, r),
        grid=(1,),
        in_specs=[_full((nb, r * 2 * c)), _full((nb, n)), _full((nb, n)),
                  _full((c, 2 * n)), _full((2 * n, c)), _full((2, n)),
                  _full((1, c)), _full((c, c)), _full((1, c))],
        out_specs=[_full((nb, r * c)), _full((nb, n)), _full((nb, n))],
        out_shape=[jax.ShapeDtypeStruct((nb, r * c), F32),
                   jax.ShapeDtypeStruct((nb, n), F32),
                   jax.ShapeDtypeStruct((nb, n), F32)],
        scratch_shapes=[pltpu.VMEM((r, nb, c), F32), pltpu.VMEM((r, nb, c), F32),
                        pltpu.VMEM((r, nb, 2 * n), F32), pltpu.VMEM((r, nb, 2 * n), BF16)],
        compiler_params=_cparams(1),
        name="ssm_sample",
    )(zbv, s0_re, s0_im, bmat, cmat, ab, d, gw, gb)
    return yb.reshape(nb * seq_len, c), s_re, s_im


def _prompt_front_kernel(n_blk, nr, x_ref, g_ref, w_ref, qg_ref, kg_ref, e_ref,
                         cw_ref, cb_ref, wg_ref, bg_ref, lam_ref,
                         bmat_ref, cmat_ref, ab_ref, apow_ref, d_ref, gw_ref, gb_ref,
                         q_ref, k_ref, v_ref, gc_ref, kf_ref, vf_ref,
                         ya_ref, conv_ref, hl_ref, yb_ref, sre_ref, sim_ref,
                         hn_ref, za_ref, zb_ref,
                         xc_ref, a_ref, b_ref, cina_ref, cx_ref, ch_ref,
                         u_ref, g2_ref, xs_ref, xb_ref, cinb_ref, cst_ref):
    _lru_prompt_init(cx_ref, ch_ref)
    _ssm_prompt_init(cst_ref)
    _inproj_kernel(n_blk, x_ref, g_ref, w_ref, qg_ref, kg_ref, e_ref,
                   za_ref, zb_ref, q_ref, k_ref, v_ref, gc_ref, kf_ref, vf_ref, hn_ref)
    _lru_prompt_main(n_blk, nr, za_ref, cw_ref, cb_ref, wg_ref, bg_ref, lam_ref, ya_ref,
                     xc_ref, a_ref, b_ref, cina_ref, cx_ref, ch_ref)
    _ssm_prompt_main(n_blk, nr, zb_ref, bmat_ref, cmat_ref, ab_ref, apow_ref, d_ref, gw_ref, gb_ref,
                     yb_ref, u_ref, g2_ref, xs_ref, xb_ref, cinb_ref, cst_ref)
    _lru_prompt_finish(n_blk, nr, za_ref, conv_ref, hl_ref, ch_ref)
    _ssm_prompt_finish(sre_ref, sim_ref, cst_ref)


def _prompt_front(x3, keep_rows, layer, kv_stack, norm_g, w_in_bf, qg, kg, emat, cw, cb, wg, bg, lam,
                  bmat, cmat, ab, apow, d, gw, gb):
    r, nr, n = CHUNK, CHUNK_ROWS, N_STATE
    nb, nrows, _ = x3.shape
    assert nrows % nr == 0 and keep_rows % nr == 0
    skip = (nrows - keep_rows) // nr
    kv_spec = pl.BlockSpec((None, None, nr, r * D_ATT),
                           lambda b, t: (layer, b, jnp.maximum(t - skip, 0), 0))
    n_in = 18
    stack_in = list(kv_stack)

    def body(*refs):
        _prompt_front_kernel(r, nr, *refs[:n_in], *refs[n_in + len(stack_in):])

    def per_seq(rows, width):
        return pl.BlockSpec((None, rows, width), lambda b, t: (b, 0, 0))

    def out(c, dtype=F32):
        return jax.ShapeDtypeStruct((nb, nrows, r * c), dtype)

    def seq_out(rows, width):
        return jax.ShapeDtypeStruct((nb, rows, width), F32)

    kv_shape = jax.ShapeDtypeStruct((DEPTH, nb, keep_rows, r * D_ATT), F32)
    att = _tile3(nr, r * D_ATT)
    res = pl.pallas_call(
        body,
        grid=(nb, nrows // nr),
        in_specs=[_tile3(nr, r * D_MODEL), _full((1, D_MODEL)), _full((D_MODEL, D_IN)),
                  _full((1, D_ATT)), _full((1, D_ATT)), _full((D_ATT, D_ATT)),
                  _full((CONV_W, D_LRU)), _full((1, D_LRU)), _full((D_LRU, 2 * D_LRU)),
                  _full((1, 2 * D_LRU)), _full((1, D_LRU)),
                  _full((D_SSM, 2 * n)), _full((2 * n, D_SSM)), _full((2, n)), _full((r, 2 * n)),
                  _full((1, D_SSM)), _full((D_SSM, D_SSM)), _full((1, D_SSM))]
                 + [pl.BlockSpec(memory_space=pl.ANY)] * len(stack_in),
        input_output_aliases={n_in + i: 4 + i for i in range(len(stack_in))},
        out_specs=[att, att, att, att, kv_spec, kv_spec,
                   _tile3(nr, r * D_LRU), per_seq(CONV_W - 1, D_LRU), per_seq(1, D_LRU),
                   _tile3(nr, r * D_SSM), per_seq(1, n), per_seq(1, n)],
        out_shape=[out(D_ATT, BF16), out(D_ATT, BF16), out(D_ATT, BF16), out(D_ATT), kv_shape, kv_shape,
                   out(D_LRU), seq_out(CONV_W - 1, D_LRU), seq_out(1, D_LRU),
                   out(D_SSM), seq_out(1, n), seq_out(1, n)],
        scratch_shapes=[pltpu.VMEM((r * nr, D_MODEL), BF16),
                        pltpu.VMEM((nr, r * 2 * D_LRU), F32), pltpu.VMEM((nr, r * 2 * D_SSM), F32),
                        pltpu.VMEM((r, nr, D_LRU), F32), pltpu.VMEM((r, nr, D_LRU), F32),
                        pltpu.VMEM((r, nr, D_LRU), F32), pltpu.VMEM((nr, D_LRU), F32),
                        pltpu.VMEM((8, D_LRU), F32), pltpu.VMEM((8, D_LRU), F32),
                        pltpu.VMEM((r, nr, D_SSM), F32), pltpu.VMEM((r, nr, D_SSM), F32),
                        pltpu.VMEM((r, nr, 2 * n), F32), pltpu.VMEM((r, nr, 2 * n), BF16),
                        pltpu.VMEM((nr, 2 * n), F32), pltpu.VMEM((8, 2 * n), F32)],
        compiler_params=_cparams(2),
        name="prompt_front",
    )(x3, norm_g, w_in_bf, qg, kg, emat, cw, cb, wg, bg, lam, bmat, cmat, ab, apow, d, gw, gb, *stack_in)
    q, k, v, gc, k_stack, v_stack, ya, conv_new, h_last, yb, s_re, s_im = res
    return (q, k, v, gc, (k_stack, v_stack), ya, conv_new, h_last.reshape(nb, D_LRU),
            yb, s_re.reshape(nb, n), s_im.reshape(nb, n))


def _attend(q, kk, vv, bias_ref, first):
    blk = q.shape[0]
    lane = lax.broadcasted_iota(jnp.int32, (1, LANES), 1)
    outs, lses = [], []
    for pair in range(N_HEADS // 2):
        sl = slice(pair * LANES, (pair + 1) * LANES)
        qp, kp, vp = q[:, sl], kk[:, sl], vv[:, sl]
        acc = jnp.zeros((blk, LANES), F32)
        lse = jnp.zeros((blk, LANES), F32)
        for hh in range(2):
            head_lanes = (lane >= HEAD_DIM) if hh else (lane < HEAD_DIM)
            qh = jnp.where(head_lanes, qp, jnp.zeros_like(qp))
            s = lax.dot_general(qh, kp, (((1,), (1,)), ((), ())), preferred_element_type=F32)
            s = s + bias_ref[2 * pair + hh + N_HEADS * first]
            m = jnp.max(s, axis=-1, keepdims=True)
            e = jnp.exp2(s - m)
            den = jnp.sum(e, axis=-1, keepdims=True)
            vh = jnp.where(head_lanes, vp, jnp.zeros_like(vp))
            pv = jnp.dot(e.astype(BF16), vh, preferred_element_type=F32)
            acc = acc + pv * (1.0 / den)
            lse = jnp.where(head_lanes, m + jnp.log2(den), lse)
        outs.append(acc)
        lses.append(lse)
    return outs, lses


SUB_ROWS = ATT_BLOCK // (CHUNK // 4)
D1_ROWS = 2 * ATT_BLOCK // CHUNK
HALF_ROWS = D1_ROWS // 2
D16_UNROLL = 4
SEQ_PER_STEP = CHUNK // D16_UNROLL + 3 * (ATT_BLOCK // SUB_ROWS)


def _put_pairs(dst, lead, rows, vals, src_rows=None):
    for pair in range(N_HEADS // 2):
        v = vals[pair] if src_rows is None else vals[pair][src_rows]
        dst[lead, rows, pair * LANES:(pair + 1) * LANES] = v


def _cache_copy(hbm, buf, sem, layer, seq, slot, which):
    return pltpu.make_async_copy(hbm.at[layer, seq], buf.at[slot], sem.at[which, slot])


def _attn_kernel(layer, n_seq, t_new,
                 q_ref, k_ref, v_ref, b16_ref, b4_ref, b1_ref,
                 qs_ref, kn_ref, vn_ref, kt_hbm, vt_hbm, sbias_ref,
                 o_ref, os_ref,
                 qst, kst, vst, o16, l16, o4, l4, o1, l1, qq, kk, vv, qf, kf, vf, qq1, kk1, vv1,
                 kbuf, vbuf, sem):
    blk, c = ATT_BLOCK, D_ATT
    jb = pl.program_id(1)
    step = pl.program_id(0) * pl.num_programs(1) + jb
    n_steps = pl.num_programs(0) * pl.num_programs(1)
    first = jnp.where(jb == 0, 1, 0)

    @pl.when(step == 0)
    def _():
        _cache_copy(kt_hbm, kbuf, sem, layer, 0, 0, 0).start()
        _cache_copy(vt_hbm, vbuf, sem, layer, 0, 0, 1).start()

    @pl.when(jb == 0)
    def _():
        for s in range(CHUNK):
            kst[s, 0:blk, :] = jnp.zeros((blk, c), BF16)
            vst[s, 0:blk, :] = jnp.zeros((blk, c), BF16)

    for s in range(CHUNK):
        sl = slice(s * c, (s + 1) * c)
        qst[s] = q_ref[:, sl]
        kst[s, blk:2 * blk, :] = k_ref[:, sl]
        vst[s, blk:2 * blk, :] = v_ref[:, sl]

    rows_qh = N_HEADS * t_new
    row_head = lax.broadcasted_iota(jnp.int32, (rows_qh, c), 0) // t_new
    lane_head = lax.broadcasted_iota(jnp.int32, (rows_qh, c), 1) // HEAD_DIM
    own = row_head == lane_head
    nt = (((1,), (1,)), ((), ()))

    def sample_seq(local):
        g = step * SEQ_PER_STEP + local
        slot = lax.rem(g, 2)
        nxt = jnp.minimum(g + 1, n_seq - 1)
        _cache_copy(kt_hbm, kbuf, sem, layer, nxt, 1 - slot, 0).start()
        _cache_copy(vt_hbm, vbuf, sem, layer, nxt, 1 - slot, 1).start()
        _cache_copy(kt_hbm, kbuf, sem, layer, g, slot, 0).wait()
        _cache_copy(vt_hbm, vbuf, sem, layer, g, slot, 1).wait()
        kt = kbuf[slot].reshape(c, PAST_LEN).astype(BF16)
        vt = vbuf[slot].reshape(c, PAST_LEN).astype(BF16)
        pad = jnp.zeros((LANES - t_new, c), F32)
        kn = jnp.concatenate([kn_ref[local], pad], axis=0).astype(BF16)
        vn = jnp.concatenate([vn_ref[local], pad], axis=0).astype(BF16)
        q8 = qs_ref[local].astype(F32)
        qbd = jnp.where(own, jnp.concatenate([q8] * N_HEADS, axis=0), 0.0).astype(BF16)
        s = jnp.concatenate([jnp.dot(qbd, kt, preferred_element_type=F32),
                             lax.dot_general(qbd, kn, nt, preferred_element_type=F32)], axis=1)
        es, dens, lses = [], [], []
        for p, (window, _) in enumerate(DILATED):
            lo = PAST_LEN - window
            sp = s[:, lo:] + sbias_ref[p, :, lo:]
            m = jnp.max(sp, axis=-1, keepdims=True)
            e = jnp.exp2(sp - m)
            den = jnp.sum(e, axis=-1, keepdims=True)
            es.append(e)
            dens.append(den)
            lses.append(m + jnp.log2(den))
        top = jnp.maximum(jnp.maximum(lses[0], lses[1]), lses[2])
        ws = [jnp.exp2(l - top) for l in lses]
        wsum = ws[0] + ws[1] + ws[2]
        cs = [w / (den * wsum) for w, den in zip(ws, dens)]
        cut1, cut2 = PAST_LEN - DILATED[1][0], PAST_LEN - DILATED[0][0]
        far = cs[2] * es[2][:, :cut1]
        mid = cs[2] * es[2][:, cut1:cut2] + cs[1] * es[1][:, :cut2 - cut1]
        near = cs[2] * es[2][:, cut2:] + cs[1] * es[1][:, cut2 - cut1:] + cs[0] * es[0]
        pc = jnp.concatenate([far, mid, near], axis=1).astype(BF16)
        o_all = (lax.dot_general(pc[:, :PAST_LEN], vt, nt, preferred_element_type=F32)
                 + jnp.dot(pc[:, PAST_LEN:], vn, preferred_element_type=F32))
        o_all = jnp.where(own, o_all, 0.0)
        out = o_all[0:t_new]
        for h in range(1, N_HEADS):
            out = out + o_all[h * t_new:(h + 1) * t_new]
        os_ref[local] = out

    def d16_body(i, carry):
        for u in range(D16_UNROLL):
            s = i * D16_UNROLL + u
            outs, lses = _attend(qst.at[s], kst.at[s], vst.at[s], b16_ref, first)
            _put_pairs(o16, s, slice(None), outs)
            _put_pairs(l16, s, slice(None), lses)
        sample_seq(i)
        return carry

    n_d16 = CHUNK // D16_UNROLL
    lax.fori_loop(0, n_d16, d16_body, 0)

    def sub_body(j, carry):
        r0 = pl.multiple_of(j * SUB_ROWS, SUB_ROWS)
        first_j = first * jnp.where(j == 0, 1, 0)
        seq0 = n_d16 + 3 * j

        for r in range(4):
            for m in range(CHUNK // 4):
                s = r + 4 * m
                qq[m * SUB_ROWS:(m + 1) * SUB_ROWS, :] = qst[s, pl.ds(r0, SUB_ROWS), :]
                kk[2 * m * SUB_ROWS:2 * (m + 1) * SUB_ROWS, :] = kst[s, pl.ds(blk - SUB_ROWS + r0, 2 * SUB_ROWS), :]
                vv[2 * m * SUB_ROWS:2 * (m + 1) * SUB_ROWS, :] = vst[s, pl.ds(blk - SUB_ROWS + r0, 2 * SUB_ROWS), :]
            outs, lses = _attend(qq, kk, vv, b4_ref, first_j)
            for m in range(CHUNK // 4):
                _put_pairs(o4, r + 4 * m, slice(None), outs, slice(m * SUB_ROWS, (m + 1) * SUB_ROWS))
                _put_pairs(l4, r + 4 * m, slice(None), lses, slice(m * SUB_ROWS, (m + 1) * SUB_ROWS))
        sample_seq(seq0)

        for t2 in range(SUB_ROWS // D1_ROWS):
            rt = r0 + t2 * D1_ROWS
            for s in range(CHUNK):
                qp = qst[s, pl.ds(pl.multiple_of(rt, D1_ROWS), D1_ROWS), :].astype(F32)
                for part in range(2):
                    qf[part * blk + s * HALF_ROWS:part * blk + (s + 1) * HALF_ROWS, :] = (
                        qp[part * HALF_ROWS:(part + 1) * HALF_ROWS])
                for src, dst in ((kst, kf), (vst, vf)):
                    kp = src[s, pl.ds(pl.multiple_of(blk - D1_ROWS + rt, D1_ROWS), 2 * D1_ROWS), :].astype(F32)
                    for part in range(3):
                        dst[part * blk + s * HALF_ROWS:part * blk + (s + 1) * HALF_ROWS, :] = (
                            kp[(part + 1) * HALF_ROWS:(part + 2) * HALF_ROWS])
            qq1[...] = qf[...].astype(BF16)
            kk1[...] = kf[...].astype(BF16)
            vv1[...] = vf[...].astype(BF16)
            for part in range(2):
                sel = first_j if (t2 == 0 and part == 0) else 0
                outs, lses = _attend(qq1.at[pl.ds(part * blk, blk), :], kk1.at[pl.ds(part * blk, 2 * blk), :],
                                     vv1.at[pl.ds(part * blk, 2 * blk), :], b1_ref, sel)
                rows = slice(t2 * D1_ROWS + part * HALF_ROWS, t2 * D1_ROWS + (part + 1) * HALF_ROWS)
                for s in range(CHUNK):
                    _put_pairs(o1, s, rows, outs, slice(s * HALF_ROWS, (s + 1) * HALF_ROWS))
                    _put_pairs(l1, s, rows, lses, slice(s * HALF_ROWS, (s + 1) * HALF_ROWS))
        sample_seq(seq0 + 1)

        for s in range(CHUNK):
            for lo in range(0, c, LANES):
                ls = [l16[s, pl.ds(r0, SUB_ROWS), lo:lo + LANES], l4[s, :, lo:lo + LANES], l1[s, :, lo:lo + LANES]]
                os = [o16[s, pl.ds(r0, SUB_ROWS), lo:lo + LANES], o4[s, :, lo:lo + LANES], o1[s, :, lo:lo + LANES]]
                top = jnp.maximum(jnp.maximum(ls[0], ls[1]), ls[2])
                ws = [jnp.exp2(l - top) for l in ls]
                mix = (ws[0] * os[0] + ws[1] * os[1] + ws[2] * os[2]) / (ws[0] + ws[1] + ws[2])
                o_ref[pl.ds(r0, SUB_ROWS), s * c + lo:s * c + lo + LANES] = mix
        sample_seq(seq0 + 2)
        return carry

    lax.fori_loop(0, blk // SUB_ROWS, sub_body, 0)

    for s in range(CHUNK):
        for lo in range(0, c, LANES):
            kst[s, 0:blk, lo:lo + LANES] = kst[s, blk:2 * blk, lo:lo + LANES]
            vst[s, 0:blk, lo:lo + LANES] = vst[s, blk:2 * blk, lo:lo + LANES]

    @pl.when(step == n_steps - 1)
    def _():
        spare = 1 - (n_seq - 1) % 2
        _cache_copy(kt_hbm, kbuf, sem, layer, n_seq - 1, spare, 0).wait()
        _cache_copy(vt_hbm, vbuf, sem, layer, n_seq - 1, spare, 1).wait()


def _prompt_bias(q_idx, k_idx, dil):
    dist = q_idx[:, None] - k_idx[None, :]
    valid = (dist >= 0) & (dist <= ATT_J)
    bias = -(ALIBI_SLOPES * LOG2_E)[:, None, None] * (dist * dil).astype(np.float32)[None]
    table = np.where(valid[None], bias, NEG_INF)
    masked = np.where((k_idx >= 0)[None, None, :], table, NEG_INF)
    return jnp.asarray(np.concatenate([table, masked], axis=0), dtype=F32)


def _attention(q, k, v, qs, k_new, v_new, cache_kt, cache_vt, layer):
    blk, c = ATT_BLOCK, D_ATT
    nb, nrows, width = q.shape
    n_seq = cache_kt.shape[1]
    t_new = qs.size // (n_seq * c)
    n_steps = nb * (nrows // blk)
    assert nrows % blk == 0 and n_seq == n_steps * SEQ_PER_STEP and cache_kt.shape[-1] == PAST_LEN
    a = np.arange(blk)
    c2 = np.arange(2 * blk)
    b16 = _prompt_bias(a, c2 - blk, 16)
    k4 = 4 * (c2 % SUB_ROWS + SUB_ROWS * ((c2 // SUB_ROWS) % 2 - 1)) + c2 // (2 * SUB_ROWS)
    b4 = _prompt_bias(4 * (a % SUB_ROWS) + a // SUB_ROWS, k4, 4)
    order1 = CHUNK * (a % HALF_ROWS) + a // HALF_ROWS
    b1 = _prompt_bias(order1, np.concatenate([order1 - blk, order1]), 1)
    spec = _tile3(blk, width)
    bias_spec = _full((2 * N_HEADS, blk, 2 * blk))
    seq_spec = pl.BlockSpec((SEQ_PER_STEP, t_new, c), lambda b, t: (b * (nrows // blk) + t, 0, 0))
    hbm_spec = pl.BlockSpec(memory_space=pl.ANY)

    def st(rows, dtype):
        return pltpu.VMEM((CHUNK, rows, c), dtype)

    o, o_s = pl.pallas_call(
        functools.partial(_attn_kernel, layer, n_seq, t_new),
        grid=(nb, nrows // blk),
        in_specs=[spec, spec, spec, bias_spec, bias_spec, bias_spec,
                  seq_spec, seq_spec, seq_spec, hbm_spec, hbm_spec,
                  _full((len(DILATED), N_HEADS * t_new, PAST_LEN + LANES))],
        out_specs=[spec, seq_spec],
        out_shape=[jax.ShapeDtypeStruct((nb, nrows, width), F32),
                   jax.ShapeDtypeStruct((n_seq, t_new, c), F32)],
        scratch_shapes=[st(blk, BF16), st(2 * blk, BF16), st(2 * blk, BF16),
                        st(blk, F32), st(blk, F32),
                        st(SUB_ROWS, F32), st(SUB_ROWS, F32), st(SUB_ROWS, F32), st(SUB_ROWS, F32),
                        pltpu.VMEM((blk, c), BF16), pltpu.VMEM((2 * blk, c), BF16), pltpu.VMEM((2 * blk, c), BF16),
                        pltpu.VMEM((2 * blk, c), F32), pltpu.VMEM((3 * blk, c), F32), pltpu.VMEM((3 * blk, c), F32),
                        pltpu.VMEM((2 * blk, c), BF16), pltpu.VMEM((3 * blk, c), BF16), pltpu.VMEM((3 * blk, c), BF16),
                        pltpu.VMEM((2, N_HEADS, HEAD_DIM, PAST_LEN), F32),
                        pltpu.VMEM((2, N_HEADS, HEAD_DIM, PAST_LEN), F32),
                        pltpu.SemaphoreType.DMA((2, 2))],
        compiler_params=_cparams(2), name="attention",
    )(q, k, v, b16, b4, b1,
      qs.reshape(n_seq, t_new, c), k_new.reshape(n_seq, t_new, c), v_new.reshape(n_seq, t_new, c),
      cache_kt, cache_vt, _sample_bias(t_new))
    return o, o_s.reshape(n_seq * t_new, c)


def _sample_bias(t_new):
    col = np.arange(PAST_LEN + LANES)
    real = col < PAST_LEN + t_new
    t = np.arange(t_new)
    dist = PAST_LEN + t[:, None] - col[None, :]
    out = []
    for window, dil in DILATED:
        valid = real[None, :] & (dist >= 0) & (dist % dil == 0) & (dist <= window)
        bias = -(ALIBI_SLOPES * LOG2_E)[:, None, None] * dist.astype(np.float32)[None]
        bias = np.where(valid[None], bias, NEG_INF)
        out.append(bias.reshape(N_HEADS * t_new, PAST_LEN + LANES))
    return jnp.asarray(np.stack(out, axis=0), dtype=F32)


def _outproj_kernel(n_blk, x_ref, ya_ref, yb_ref, gc_ref, o_ref, og_ref, w_ref, out_ref, yn_ref):
    rows = x_ref.shape[0]
    a_hi, b_hi = D_LRU, D_LRU + D_SSM

    def norm_into(y, lo, hi, s):
        yn = y * lax.rsqrt(jnp.mean(y * y, axis=-1, keepdims=True) + EPS) * og_ref[:, lo:hi]
        yn_ref[s * rows:(s + 1) * rows, lo:hi] = yn.astype(BF16)

    for s in range(n_blk):
        att = slice(s * D_ATT, (s + 1) * D_ATT)
        yc = o_ref[:, att] * _silu(gc_ref[:, att])
        norm_into(ya_ref[:, s * D_LRU:(s + 1) * D_LRU], 0, a_hi, s)
        norm_into(yb_ref[:, s * D_SSM:(s + 1) * D_SSM], a_hi, b_hi, s)
        norm_into(yc, b_hi, D_MODEL, s)
    res = jnp.dot(yn_ref[...], w_ref[...], preferred_element_type=F32)
    for s in range(n_blk):
        sl = slice(s * D_MODEL, (s + 1) * D_MODEL)
        out_ref[:, sl] = x_ref[:, sl] + res[s * rows:(s + 1) * rows]


def _outproj(x3, n_blk, rows_t, ya, yb, gc, o, og, w_out_bf):
    nb, nrows, _ = x3.shape
    att_spec = _tile3(rows_t, n_blk * D_ATT)
    return pl.pallas_call(
        functools.partial(_outproj_kernel, n_blk),
        grid=(nb, nrows // rows_t),
        in_specs=[_tile3(rows_t, n_blk * D_MODEL), _tile3(rows_t, n_blk * D_LRU),
                  _tile3(rows_t, n_blk * D_SSM), att_spec, att_spec,
                  _full((1, D_MODEL)), _full((D_MODEL, D_MODEL))],
        out_specs=_tile3(rows_t, n_blk * D_MODEL),
        out_shape=jax.ShapeDtypeStruct(x3.shape, F32),
        scratch_shapes=[pltpu.VMEM((n_blk * rows_t, D_MODEL), BF16)],
        compiler_params=_cparams(2),
        name="outproj",
    )(x3, ya, yb, gc, o, og, w_out_bf)


def _block_diag(blocks):
    k, i, j = blocks.shape
    eye = jnp.eye(k, dtype=blocks.dtype)
    return jnp.einsum("kij,kl->kilj", blocks, eye).reshape(k * i, k * j)


def kernel(x_prompt, x_sample, state_conv, state_lru, state_ssm_re, state_ssm_im, cache_k, cache_v,
           norm_g, w_in, conv_w, conv_b, w_r, b_r, w_i, b_i, lru_lambda,
           ssm_lambda_re, ssm_lambda_im, ssm_log_dt, ssm_b_re, ssm_b_im, ssm_c_re, ssm_c_im,
           ssm_d, glu_w, glu_b, q_norm_g, k_norm_g, out_norm_g, w_out):
    bp, seq, _ = x_prompt.shape
    bs_, t_new, _ = x_sample.shape
    keep = min(PAST_LEN, seq)
    assert seq % (CHUNK * PROJ_ROWS) == 0 and keep % (CHUNK * PROJ_ROWS) == 0
    n_s = bs_ * t_new
    rows_s = min(ROW_TILE, n_s)
    xp = x_prompt.reshape(bp, seq // CHUNK, CHUNK * D_MODEL)
    xs = x_sample.reshape(1, n_s, D_MODEL)
    cache_kt = jnp.transpose(cache_k, (0, 1, 3, 4, 2))
    cache_vt = jnp.transpose(cache_v, (0, 1, 3, 4, 2))

    head_of = jnp.arange(D_ATT) // HEAD_DIM
    emat = (head_of[:, None] == head_of[None, :]).astype(BF16)

    def chan_major(b):
        return jnp.transpose(b, (0, 3, 1, 2)).reshape(DEPTH, SSM_GROUP, N_STATE)

    def vec(a):
        return a.reshape(DEPTH, 1, N_STATE)

    ab_all, bbr_all, bbi_all, apow_all = _ssm_discretise(
        vec(ssm_lambda_re), vec(ssm_lambda_im), vec(ssm_log_dt), chan_major(ssm_b_re), chan_major(ssm_b_im))

    sp_list, ss_list = [], []
    kv_stack = tuple(jnp.zeros((DEPTH, bp, keep // CHUNK, CHUNK * D_ATT), F32) for _ in range(2))
    for l in range(DEPTH):
        w_in_bf = w_in[l].astype(BF16)
        w_out_bf = w_out[l].astype(BF16)
        ng = norm_g[l].reshape(1, D_MODEL)
        qg = jnp.tile(q_norm_g[l], N_HEADS).reshape(1, D_ATT)
        kg = jnp.tile(k_norm_g[l], N_HEADS).reshape(1, D_ATT)
        og = out_norm_g[l].reshape(1, D_MODEL)
        cw, cb = conv_w[l], conv_b[l].reshape(1, D_LRU)
        wg = jnp.concatenate([_block_diag(w_r[l]), _block_diag(w_i[l])], axis=1).astype(BF16)
        bg = jnp.concatenate([b_r[l], b_i[l]]).reshape(1, 2 * D_LRU)
        lam = lru_lambda[l].reshape(1, D_LRU)

        def b_matrix(bb):
            blocks = jnp.transpose(bb.reshape(SSM_GROUP, SSM_GROUPS, SSM_STATE), (1, 0, 2))
            return _block_diag(blocks)
        bmat = jnp.concatenate([b_matrix(bbr_all[l]), b_matrix(bbi_all[l])], axis=1).astype(BF16)
        cmat = jnp.concatenate([_block_diag(jnp.transpose(ssm_c_re[l], (0, 2, 1))),
                                -_block_diag(jnp.transpose(ssm_c_im[l], (0, 2, 1)))], axis=0).astype(BF16)
        d = ssm_d[l].reshape(1, D_SSM)
        gw = glu_w[l].astype(BF16)
        gb = glu_b[l].reshape(1, D_SSM)

        q, k, v, gc, kv_stack, ya, conv_new, h_last, yb, s_re, s_im = _prompt_front(
            xp, keep // CHUNK, l, kv_stack, ng, w_in_bf, qg, kg, emat, cw, cb, wg, bg, lam,
            bmat, cmat, ab_all[l], apow_all[l], d, gw, gb)
        sp_list.append((conv_new, h_last,
                        s_re.reshape(bp, SSM_GROUPS, SSM_STATE), s_im.reshape(bp, SSM_GROUPS, SSM_STATE)))

        zas, zbs, qs, _, _, gcs, ks_rows, vs_rows = _inproj(xs, 1, rows_s, n_s, ng, w_in_bf, qg, kg, emat)
        yas, conv_new, h_last = _lru_sample(zas, bs_, t_new, state_conv[l], state_lru[l], cw, cb, wg, bg, lam)
        ybs, s_re, s_im = _ssm_sample(zbs, bs_, t_new, state_ssm_re[l].reshape(bs_, N_STATE),
                                      state_ssm_im[l].reshape(bs_, N_STATE), bmat, cmat, ab_all[l], d, gw, gb)
        ss_list.append((conv_new, h_last,
                        s_re.reshape(bs_, SSM_GROUPS, SSM_STATE), s_im.reshape(bs_, SSM_GROUPS, SSM_STATE),
                        ks_rows.reshape(bs_, t_new, N_HEADS, HEAD_DIM), vs_rows.reshape(bs_, t_new, N_HEADS, HEAD_DIM)))

        o, o_s = _attention(q, k, v, qs, ks_rows, vs_rows, cache_kt, cache_vt, l)
        xp = _outproj(xp, CHUNK, PROJ_ROWS, ya, yb, gc, o, og, w_out_bf)
        xs = _outproj(xs, 1, rows_s, yas.reshape(1, n_s, D_LRU), ybs.reshape(1, n_s, D_SSM), gcs,
                      o_s.reshape(1, n_s, D_ATT), og, w_out_bf)

    def stack(lst, i):
        return jnp.stack([s[i] for s in lst], axis=0)

    def rows_out(a):
        t = jnp.transpose(a.reshape(DEPTH, bp, keep, D_ATT), (0, 1, 3, 2))
        return jnp.transpose(t.reshape(DEPTH, bp, N_HEADS, HEAD_DIM, keep), (0, 1, 4, 2, 3))

    return (xp.reshape(bp, seq, D_MODEL), xs.reshape(bs_, t_new, D_MODEL),
            stack(sp_list, 0), stack(sp_list, 1), stack(sp_list, 2), stack(sp_list, 3),
            rows_out(kv_stack[0]), rows_out(kv_stack[1]),
            stack(ss_list, 0), stack(ss_list, 1), stack(ss_list, 2), stack(ss_list, 3),
            stack(ss_list, 4), stack(ss_list, 5))
```
